```python
import jax, jax.numpy as jnp
from jax import lax
import numpy as np

D_MODEL = 2048
BATCH = 2
SEQ = 8192
DEPTH = 4
DEC_BATCH = 32
DEC_SEQ = 64
PAST_LEN = 1024

CHUNK = 64
HEAD_DIM = D_MODEL // 16
POOL_WINDOWS = (2, 4, 8, 16)
N_POOL_GROUPS = len(POOL_WINDOWS)
POOL_WIDTH = N_POOL_GROUPS * HEAD_DIM
N_CONV_HEADS = (3 * D_MODEL // 8) // HEAD_DIM
CONV_CH = N_CONV_HEADS * HEAD_DIM
CONV_K = 3
N_SGU_HEADS = (3 * D_MODEL // 8) // HEAD_DIM
SGU_WIDTH = N_SGU_HEADS * HEAD_DIM
SGU_LEN = 128
MIX_WIDTH = POOL_WIDTH + CONV_CH + SGU_WIDTH
IN_WIDTH = POOL_WIDTH + 3 * CONV_CH + 2 * SGU_WIDTH
D_FF = ((8 * D_MODEL // 3 + 255) // 256) * 256
POOL_STATE = max(POOL_WINDOWS) - 1
CONV_STATE = CONV_K - 1
N_NORMS = 6
EPS = 1e-6

kernel_name = "hybrid_pool_conv_sgu_streaming_step"


def rmsnorm(x, g):
    xf = x.astype(jnp.float32)
    y = xf * lax.rsqrt(jnp.mean(xf * xf, axis=-1, keepdims=True) + EPS)
    return (y * g.astype(jnp.float32)).astype(x.dtype)


def rms_plain(x):
    xf = x.astype(jnp.float32)
    return (xf * lax.rsqrt(jnp.mean(xf * xf, axis=-1, keepdims=True) + EPS)).astype(x.dtype)


def swiglu(x, wg, wu, wd):
    return (jax.nn.silu(x @ wg) * (x @ wu)) @ wd


def pool_mixer(p, hist, pos0, w_grp, scale):
    B, T, _ = p.shape
    ext = p if hist is None else jnp.concatenate([hist.astype(p.dtype), p], axis=1)
    H = ext.shape[1] - T
    cs = jnp.pad(jnp.cumsum(ext.astype(jnp.float32), axis=1), ((0, 0), (1, 0), (0, 0)))
    pf = p.astype(jnp.float32)
    t = np.arange(T)
    hi = t + H + 1
    outs = []
    for g, w in enumerate(POOL_WINDOWS):
        sl = slice(g * HEAD_DIM, (g + 1) * HEAD_DIM)
        lo = np.maximum(hi - w, 0)
        cnt = np.minimum(pos0 + t + 1, w).astype(np.float32)
        win_sum = cs[:, hi, sl] - cs[:, lo, sl]
        outs.append(win_sum / cnt[None, :, None] - pf[..., sl])
    d = jnp.stack(outs, axis=2)
    y = jnp.einsum('btgc,gcd->btgd', d, w_grp.astype(jnp.float32)).reshape(B, T, POOL_WIDTH)
    y = y * scale.astype(jnp.float32)
    return y.astype(p.dtype), ext[:, -POOL_STATE:]


def conv_mixer(h, bg, cg, hist, w_conv):
    T = h.shape[1]
    z = cg * h
    ext = jnp.concatenate([hist.astype(z.dtype), z], axis=1)
    y = ext[:, 0:T] * w_conv[0] + ext[:, 1:T + 1] * w_conv[1] + ext[:, 2:T + 2] * w_conv[2]
    return bg * y, ext[:, -CONV_STATE:]


def sgu_mixer(u, v, w_s, b_s):
    B, T, _ = u.shape
    n = -(-T // SGU_LEN)
    Tp = n * SGU_LEN
    vp = jnp.pad(v, ((0, 0), (0, Tp - T), (0, 0))).reshape(B, n, SGU_LEN, N_SGU_HEADS, HEAD_DIM)
    mask = jnp.tril(jnp.ones((SGU_LEN, SGU_LEN), dtype=bool))
    ws = jnp.where(mask[None], w_s, jnp.zeros_like(w_s))
    g = jnp.einsum('hst,bcthd->bcshd', ws, vp) + b_s.T[None, None, :, :, None]
    g = g.reshape(B, Tp, SGU_WIDTH)[:, :T]
    return u * g


def trunk_layer(x, pool_hist, conv_hist, pos0, w_in, w_out, pool_w, pool_scale, conv_w,
                sgu_w, sgu_b, f1g, f1u, f1d, f2g, f2u, f2d, ng, mix_g):
    x = x + 0.5 * rmsnorm(swiglu(rmsnorm(x, ng[0]), f1g, f1u, f1d), ng[1])
    proj = rmsnorm(x, ng[2]) @ w_in
    splits = np.cumsum([POOL_WIDTH, CONV_CH, CONV_CH, CONV_CH, SGU_WIDTH])
    p, h, bg, cg, u, v = jnp.split(proj, splits, axis=-1)
    ya, pool_new = pool_mixer(p, pool_hist, pos0, pool_w, pool_scale)
    yb, conv_new = conv_mixer(h, bg, cg, conv_hist, conv_w)
    yc = sgu_mixer(u, v, sgu_w, sgu_b)
    ymix = jnp.concatenate([rms_plain(ya), rms_plain(yb), rms_plain(yc)], axis=-1) * mix_g
    x = x + rmsnorm(ymix @ w_out, ng[3])
    x = x + 0.5 * rmsnorm(swiglu(rmsnorm(x, ng[4]), f2g, f2u, f2d), ng[5])
    return x, pool_new, conv_new, v


def setup_inputs(seed: int = 0) -> dict:
    key = jax.random.key(seed)
    ks = jax.random.split(key, 20)
    nrm = lambda k, shape, s: jax.random.normal(k, shape, jnp.float32) * s
    return {
        "x_prompt": nrm(ks[0], (BATCH, SEQ, D_MODEL), 1.0),
        "x_sample": nrm(ks[1], (DEC_BATCH, DEC_SEQ, D_MODEL), 1.0),
        "state_pool": nrm(ks[2], (DEPTH, DEC_BATCH, POOL_STATE, POOL_WIDTH), 1.0),
        "state_conv": nrm(ks[3], (DEPTH, DEC_BATCH, CONV_STATE, CONV_CH), 1.0),
        "w_in": nrm(ks[4], (DEPTH, D_MODEL, IN_WIDTH), D_MODEL ** -0.5),
        "w_out": nrm(ks[5], (DEPTH, MIX_WIDTH, D_MODEL), MIX_WIDTH ** -0.5),
        "pool_w": nrm(ks[6], (DEPTH, N_POOL_GROUPS, HEAD_DIM, HEAD_DIM), HEAD_DIM ** -0.5),
        "pool_scale": 1.0 + nrm(ks[7], (DEPTH, POOL_WIDTH), 0.1),
        "conv_w": nrm(ks[8], (DEPTH, CONV_K, CONV_CH), CONV_K ** -0.5),
        "sgu_w": nrm(ks[9], (DEPTH, N_SGU_HEADS, SGU_LEN, SGU_LEN), SGU_LEN ** -0.5),
        "sgu_b": 1.0 + nrm(ks[10], (DEPTH, N_SGU_HEADS, SGU_LEN), 0.1),
        "ffn1_gate": nrm(ks[11], (DEPTH, D_MODEL, D_FF), D_MODEL ** -0.5),
        "ffn1_up": nrm(ks[12], (DEPTH, D_MODEL, D_FF), D_MODEL ** -0.5),
        "ffn1_down": nrm(ks[13], (DEPTH, D_FF, D_MODEL), D_FF ** -0.5),
        "ffn2_gate": nrm(ks[14], (DEPTH, D_MODEL, D_FF), D_MODEL ** -0.5),
        "ffn2_up": nrm(ks[15], (DEPTH, D_MODEL, D_FF), D_MODEL ** -0.5),
        "ffn2_down": nrm(ks[16], (DEPTH, D_FF, D_MODEL), D_FF ** -0.5),
        "norm_gains": 1.0 + nrm(ks[17], (DEPTH, N_NORMS, D_MODEL), 0.1),
        "mix_gain": 1.0 + nrm(ks[18], (DEPTH, MIX_WIDTH), 0.1),
    }


def reference(x_prompt, x_sample, state_pool, state_conv, w_in, w_out, pool_w, pool_scale,
              conv_w, sgu_w, sgu_b, ffn1_gate, ffn1_up, ffn1_down, ffn2_gate, ffn2_up,
              ffn2_down, norm_gains, mix_gain):
    yp, ys = x_prompt, x_sample
    pool_p, conv_p, pool_s, conv_s, v_s = [], [], [], [], []
    conv_zero = jnp.zeros((BATCH, CONV_STATE, CONV_CH), x_prompt.dtype)
    for l in range(DEPTH):
        w = (w_in[l], w_out[l], pool_w[l], pool_scale[l], conv_w[l], sgu_w[l], sgu_b[l],
             ffn1_gate[l], ffn1_up[l], ffn1_down[l], ffn2_gate[l], ffn2_up[l], ffn2_down[l],
             norm_gains[l], mix_gain[l])
        yp, pp, cp, _ = trunk_layer(yp, None, conv_zero, 0, *w)
        ys, ps, cs, vs = trunk_layer(ys, state_pool[l], state_conv[l], PAST_LEN, *w)
        pool_p.append(pp); conv_p.append(cp)
        pool_s.append(ps); conv_s.append(cs); v_s.append(vs)
    new_pool_prompt = jnp.stack(pool_p, axis=0)
    new_conv_prompt = jnp.stack(conv_p, axis=0)
    new_pool_sample = jnp.stack(pool_s, axis=0)
    new_conv_sample = jnp.stack(conv_s, axis=0)
    new_sgu_v_sample = jnp.stack(v_s, axis=0)
    return (yp, ys, new_pool_prompt, new_conv_prompt, new_pool_sample, new_conv_sample, new_sgu_v_sample)
```

```python
import functools

import jax
import jax.numpy as jnp
from jax import lax
from jax.experimental import pallas as pl
from jax.experimental.pallas import tpu as pltpu

D_MODEL = 2048
DEPTH = 4
HEAD_DIM = 128
POOL_WINDOWS = (2, 4, 8, 16)
POOL_WIDTH = len(POOL_WINDOWS) * HEAD_DIM
N_HEADS = 6
CONV_CH = N_HEADS * HEAD_DIM
SGU_WIDTH = N_HEADS * HEAD_DIM
SGU_LEN = 128
IN_WIDTH = POOL_WIDTH + 3 * CONV_CH + 2 * SGU_WIDTH
D_FF = 5632
POOL_STATE = 15
CONV_STATE = 2
N_NORMS = 6
PAST_LEN = 1024
EPS = 1e-6

_OFF_P = 0
_OFF_H = POOL_WIDTH
_OFF_B = _OFF_H + CONV_CH
_OFF_C = _OFF_B + CONV_CH
_OFF_U = _OFF_C + CONV_CH
_OFF_V = _OFF_U + SGU_WIDTH

POOL_HIST = 16
CONV_HIST = 8

V7X_VMEM_LIMIT_BYTES = 60 * 1024 * 1024

FFN_ROWS = 1024
FFN_COLS = 512
FFN_OUT_CHUNK = 512
MIX_ROWS = 256
NORM_ROWS = 16

_F32 = jnp.float32
_BF16 = jnp.bfloat16


def _norm_rows(src_ref, dst_ref, n_rows, fn):
    def body(i, carry):
        r0 = pl.multiple_of(i * NORM_ROWS, NORM_ROWS)
        xf = src_ref[pl.ds(r0, NORM_ROWS), :].astype(_F32)
        ms = jnp.mean(xf * xf, axis=-1, keepdims=True)
        dst_ref[pl.ds(r0, NORM_ROWS), :] = fn(r0, xf * lax.rsqrt(ms + EPS)).astype(dst_ref.dtype)
        return carry

    lax.fori_loop(0, n_rows // NORM_ROWS, body, 0)


def _ffn_kernel(x_ref, gpre_ref, gpost_ref, wg_ref, wu_ref, wd_ref, o_ref, xn_ref):
    f = pl.program_id(1)
    rows = x_ref.shape[0]

    @pl.when(f == 0)
    def _():
        gpre = gpre_ref[...]
        _norm_rows(x_ref, xn_ref, rows, lambda r0, y: y * gpre)
        o_ref[...] = jnp.zeros_like(o_ref)

    xn = xn_ref[...]
    gate = jnp.dot(xn, wg_ref[...], preferred_element_type=_F32)
    up = jnp.dot(xn, wu_ref[...], preferred_element_type=_F32)
    h = (gate * jax.nn.sigmoid(gate) * up).astype(_BF16)
    for n in range(0, D_MODEL, FFN_OUT_CHUNK):
        o_ref[:, n:n + FFN_OUT_CHUNK] += jnp.dot(
            h, wd_ref[:, n:n + FFN_OUT_CHUNK], preferred_element_type=_F32)

    @pl.when(f == pl.num_programs(1) - 1)
    def _():
        gpost = gpost_ref[...]

        def fin(r0, y):
            return x_ref[pl.ds(r0, NORM_ROWS), :] + 0.5 * (y * gpost)

        _norm_rows(o_ref, o_ref, rows, fin)


def _ffn(x, gains, layer, k_pre, wg, wu, wd):
    n_rows = x.shape[0]
    tm = min(FFN_ROWS, n_rows)
    grid = (n_rows // tm, D_FF // FFN_COLS)
    gain_spec = lambda k: pl.BlockSpec((None, 1, D_MODEL), lambda i, f, k=k: (layer * N_NORMS + k, 0, 0))
    return pl.pallas_call(
        _ffn_kernel,
        out_shape=jax.ShapeDtypeStruct((n_rows, D_MODEL), _F32),
        grid=grid,
        in_specs=[
            pl.BlockSpec((tm, D_MODEL), lambda i, f: (i, 0)),
            gain_spec(k_pre),
            gain_spec(k_pre + 1),
            pl.BlockSpec((None, D_MODEL, FFN_COLS), lambda i, f: (layer, 0, f)),
            pl.BlockSpec((None, D_MODEL, FFN_COLS), lambda i, f: (layer, 0, f)),
            pl.BlockSpec((None, FFN_COLS, D_MODEL), lambda i, f: (layer, f, 0)),
        ],
        out_specs=pl.BlockSpec((tm, D_MODEL), lambda i, f: (i, 0)),
        scratch_shapes=[pltpu.VMEM((tm, D_MODEL), _BF16)],
        compiler_params=pltpu.CompilerParams(
            dimension_semantics=("arbitrary", "arbitrary"),
            vmem_limit_bytes=V7X_VMEM_LIMIT_BYTES),
        name="ffn",
    )(x, gains, gains, wg, wu, wd)


def _mixer_kernel(*refs, n_seq, seq_len, pos0, carried, emit_v):
    it = iter(refs)
    x_ref = next(it)
    if not carried:
        hp_ref = next(it)
        hz_ref = next(it)
    g2_ref, g3_ref, mixg_ref = next(it), next(it), next(it)
    win_ref, wout_ref, poolw_ref, pscale_ref = next(it), next(it), next(it), next(it)
    convw_ref, sguw_ref, sgub_ref = next(it), next(it), next(it)
    o_ref, ptail_ref, ztail_ref = next(it), next(it), next(it)
    if emit_v:
        v_ref = next(it)
    xn_ref, extp_ref, extz_ref, d_ref, yb_ref, g_ref, ymix_ref = (next(it) for _ in range(7))
    if carried:
        carryp_ref, carryz_ref = next(it), next(it)

    S, T = n_seq, seq_len
    rows = S * T
    j = pl.program_id(1) if carried else 0

    g2 = g2_ref[...]
    _norm_rows(x_ref, xn_ref, rows, lambda r0, y: y * g2)
    xn = xn_ref[...]

    def proj(off, width):
        return jnp.dot(xn, win_ref[:, off:off + width], preferred_element_type=_F32)

    p = proj(_OFF_P, POOL_WIDTH)
    if carried:
        @pl.when(j == 0)
        def _():
            carryp_ref[...] = jnp.zeros_like(carryp_ref)
            carryz_ref[...] = jnp.zeros_like(carryz_ref)

    for s in range(S):
        extp_ref[s, 0:POOL_HIST, :] = carryp_ref[...] if carried else hp_ref[s]
        extp_ref[s, POOL_HIST:POOL_HIST + T, :] = p[s * T:(s + 1) * T, :]
    t_idx = lax.broadcasted_iota(jnp.int32, (T, 1), 0) + (pos0 + j * T)
    for s in range(S):
        for g, w in enumerate(POOL_WINDOWS):
            cols = slice(g * HEAD_DIM, (g + 1) * HEAD_DIM)
            acc = extp_ref[s, POOL_HIST:POOL_HIST + T, cols]
            for k in range(1, w):
                acc = acc + extp_ref[s, POOL_HIST - k:POOL_HIST - k + T, cols]
            cnt = jnp.minimum(t_idx + 1, w).astype(_F32)
            d_ref[s * T:(s + 1) * T, cols] = acc / cnt - extp_ref[s, POOL_HIST:POOL_HIST + T, cols]
        ptail_ref[s] = extp_ref[s, T:T + POOL_HIST, :]
    if carried:
        carryp_ref[...] = extp_ref[0, T:T + POOL_HIST, :]
    pscale = pscale_ref[...]
    for g in range(len(POOL_WINDOWS)):
        cols = slice(g * HEAD_DIM, (g + 1) * HEAD_DIM)
        ya_g = jnp.dot(d_ref[:, cols].astype(_BF16), poolw_ref[g], preferred_element_type=_F32)
        d_ref[:, cols] = ya_g * pscale[:, cols]
    ya = d_ref[...]
    ymix_ref[:, 0:POOL_WIDTH] = (
        ya * lax.rsqrt(jnp.mean(ya * ya, axis=-1, keepdims=True) + EPS) * mixg_ref[:, 0:POOL_WIDTH]
    ).astype(_BF16)

    z = proj(_OFF_C, CONV_CH) * proj(_OFF_H, CONV_CH)
    for s in range(S):
        extz_ref[s, 0:CONV_HIST, :] = carryz_ref[...] if carried else hz_ref[s]
        extz_ref[s, CONV_HIST:CONV_HIST + T, :] = z[s * T:(s + 1) * T, :]
    w0, w1, w2 = convw_ref[0:1, :], convw_ref[1:2, :], convw_ref[2:3, :]
    for s in range(S):
        yb_ref[s * T:(s + 1) * T, :] = (
            extz_ref[s, CONV_HIST - 2:CONV_HIST - 2 + T, :] * w0
            + extz_ref[s, CONV_HIST - 1:CONV_HIST - 1 + T, :] * w1
            + extz_ref[s, CONV_HIST:CONV_HIST + T, :] * w2)
        ztail_ref[s] = extz_ref[s, T:T + CONV_HIST, :]
    if carried:
        carryz_ref[...] = extz_ref[0, T:T + CONV_HIST, :]
    yb = proj(_OFF_B, CONV_CH) * yb_ref[...]
    c0 = POOL_WIDTH
    ymix_ref[:, c0:c0 + CONV_CH] = (
        yb * lax.rsqrt(jnp.mean(yb * yb, axis=-1, keepdims=True) + EPS) * mixg_ref[:, c0:c0 + CONV_CH]
    ).astype(_BF16)

    v = proj(_OFF_V, SGU_WIDTH)
    if emit_v:
        v_ref[...] = v
    vb = v.astype(_BF16)
    L = min(T, SGU_LEN)
    tril = (lax.broadcasted_iota(jnp.int32, (L, L), 0) >= lax.broadcasted_iota(jnp.int32, (L, L), 1))
    for hd in range(N_HEADS):
        cols = slice(hd * HEAD_DIM, (hd + 1) * HEAD_DIM)
        w_s = jnp.where(tril, sguw_ref[hd, 0:L, 0:L], 0.0).astype(_BF16)
        bias = sgub_ref[0:L, cols]
        for r0 in range(0, rows, L):
            g_ref[r0:r0 + L, cols] = jnp.dot(w_s, vb[r0:r0 + L, cols], preferred_element_type=_F32) + bias
    yc = proj(_OFF_U, SGU_WIDTH) * g_ref[...]
    c0 = POOL_WIDTH + CONV_CH
    ymix_ref[:, c0:c0 + SGU_WIDTH] = (
        yc * lax.rsqrt(jnp.mean(yc * yc, axis=-1, keepdims=True) + EPS) * mixg_ref[:, c0:c0 + SGU_WIDTH]
    ).astype(_BF16)

    o_ref[...] = jnp.dot(ymix_ref[...], wout_ref[...], preferred_element_type=_F32)
    g3 = g3_ref[...]

    def fin(r0, y):
        return x_ref[pl.ds(r0, NORM_ROWS), :] + y * g3

    _norm_rows(o_ref, o_ref, rows, fin)


def _mixer(x, hist_p, hist_z, layer, gains, mixg, win, wout, poolw, pscale, convw, sguw, sgub,
           *, n_batch, seq_len, pos0):
    carried = hist_p is None
    n_rows = x.shape[0]
    if carried:
        T = MIX_ROWS
        S = 1
        tiles_per_seq = seq_len // T
        grid = (n_batch, tiles_per_seq)
        row_map = lambda b, j: (b * tiles_per_seq + j, 0)
        seq_map = lambda b, j: (b, 0, 0)
    else:
        T = seq_len
        S = MIX_ROWS // T
        grid = (n_batch // S,)
        row_map = lambda i: (i, 0)
        seq_map = lambda i: (i, 0, 0)
    nd = len(grid)
    rows = S * T

    def const_spec(shape, idx):
        return pl.BlockSpec(shape, lambda *_: idx, pipeline_mode=pl.Buffered(1))

    in_specs = [pl.BlockSpec((rows, D_MODEL), row_map)]
    args = [x]
    if not carried:
        in_specs += [pl.BlockSpec((S, POOL_HIST, POOL_WIDTH), seq_map),
                     pl.BlockSpec((S, CONV_HIST, CONV_CH), seq_map)]
        args += [hist_p, hist_z]
    in_specs += [
        const_spec((None, 1, D_MODEL), (layer * N_NORMS + 2, 0, 0)),
        const_spec((None, 1, D_MODEL), (layer * N_NORMS + 3, 0, 0)),
        const_spec((None, 1, D_MODEL), (layer, 0, 0)),
        const_spec((None, D_MODEL, IN_WIDTH), (layer, 0, 0)),
        const_spec((None, D_MODEL, D_MODEL), (layer, 0, 0)),
        const_spec((None, len(POOL_WINDOWS), HEAD_DIM, HEAD_DIM), (layer, 0, 0, 0)),
        const_spec((None, 1, POOL_WIDTH), (layer, 0, 0)),
        const_spec((None, CONV_HIST, CONV_CH), (layer, 0, 0)),
        const_spec((None, N_HEADS, SGU_LEN, SGU_LEN), (layer, 0, 0, 0)),
        const_spec((None, SGU_LEN, SGU_WIDTH), (layer, 0, 0)),
    ]
    args += [gains, gains, mixg, win, wout, poolw, pscale, convw, sguw, sgub]

    out_shape = [jax.ShapeDtypeStruct((n_rows, D_MODEL), _F32),
                 jax.ShapeDtypeStruct((n_batch, POOL_HIST, POOL_WIDTH), _F32),
                 jax.ShapeDtypeStruct((n_batch, CONV_HIST, CONV_CH), _F32)]
    out_specs = [pl.BlockSpec((rows, D_MODEL), row_map),
                 pl.BlockSpec((S, POOL_HIST, POOL_WIDTH), seq_map),
                 pl.BlockSpec((S, CONV_HIST, CONV_CH), seq_map)]
    emit_v = not carried
    if emit_v:
        out_shape.append(jax.ShapeDtypeStruct((n_rows, SGU_WIDTH), _F32))
        out_specs.append(pl.BlockSpec((rows, SGU_WIDTH), row_map))

    scratch = [
        pltpu.VMEM((rows, D_MODEL), _BF16),
        pltpu.VMEM((S, POOL_HIST + T, POOL_WIDTH), _F32),
        pltpu.VMEM((S, CONV_HIST + T, CONV_CH), _F32),
        pltpu.VMEM((rows, POOL_WIDTH), _F32),
        pltpu.VMEM((rows, CONV_CH), _F32),
        pltpu.VMEM((rows, SGU_WIDTH), _F32),
        pltpu.VMEM((rows, D_MODEL), _BF16),
    ]
    if carried:
        scratch += [pltpu.VMEM((POOL_HIST, POOL_WIDTH), _F32), pltpu.VMEM((CONV_HIST, CONV_CH), _F32)]

    kern = functools.partial(_mixer_kernel, n_seq=S, seq_len=T, pos0=pos0, carried=carried, emit_v=emit_v)
    return pl.pallas_call(
        kern,
        out_shape=out_shape,
        grid=grid,
        in_specs=in_specs,
        out_specs=out_specs,
        scratch_shapes=scratch,
        compiler_params=pltpu.CompilerParams(
            dimension_semantics=("arbitrary",) * nd,
            vmem_limit_bytes=V7X_VMEM_LIMIT_BYTES),
        name="mixer_prompt" if carried else "mixer_sample",
    )(*args)


def kernel(x_prompt, x_sample, state_pool, state_conv, w_in, w_out, pool_w, pool_scale, conv_w, sgu_w,
           sgu_b, ffn1_gate, ffn1_up, ffn1_down, ffn2_gate, ffn2_up, ffn2_down, norm_gains, mix_gain):
    batch, seq, _ = x_prompt.shape
    dec_batch, dec_seq, _ = x_sample.shape

    bf = lambda w: w.astype(_BF16)
    win_b, wout_b, poolw_b = bf(w_in), bf(w_out), bf(pool_w)
    f1g, f1u, f1d = bf(ffn1_gate), bf(ffn1_up), bf(ffn1_down)
    f2g, f2u, f2d = bf(ffn2_gate), bf(ffn2_up), bf(ffn2_down)
    gains = norm_gains.reshape(DEPTH * N_NORMS, 1, D_MODEL)
    mixg = mix_gain.reshape(DEPTH, 1, D_MODEL)
    pscale = pool_scale.reshape(DEPTH, 1, POOL_WIDTH)
    convw = jnp.pad(conv_w, ((0, 0), (0, CONV_HIST - conv_w.shape[1]), (0, 0)))
    sgub = jnp.repeat(jnp.swapaxes(sgu_b, 1, 2), HEAD_DIM, axis=2)
    hist_p = jnp.pad(state_pool, ((0, 0), (0, 0), (POOL_HIST - POOL_STATE, 0), (0, 0)))
    hist_z = jnp.pad(state_conv, ((0, 0), (0, 0), (CONV_HIST - CONV_STATE, 0), (0, 0)))

    yp = x_prompt.reshape(batch * seq, D_MODEL)
    ys = x_sample.reshape(dec_batch * dec_seq, D_MODEL)
    pool_p, conv_p, pool_s, conv_s, v_s = [], [], [], [], []
    for l in range(DEPTH):
        mix_w = (gains, mixg, win_b, wout_b, poolw_b, pscale, convw, sgu_w, sgub)
        yp = _ffn(yp, gains, l, 0, f1g, f1u, f1d)
        ys = _ffn(ys, gains, l, 0, f1g, f1u, f1d)
        yp, pt, zt = _mixer(yp, None, None, l, *mix_w, n_batch=batch, seq_len=seq, pos0=0)
        ys, pts, zts, vs = _mixer(ys, hist_p[l], hist_z[l], l, *mix_w,
                                  n_batch=dec_batch, seq_len=dec_seq, pos0=PAST_LEN)
        yp = _ffn(yp, gains, l, 4, f2g, f2u, f2d)
        ys = _ffn(ys, gains, l, 4, f2g, f2u, f2d)
        pool_p.append(pt[:, POOL_HIST - POOL_STATE:])
        conv_p.append(zt[:, CONV_HIST - CONV_STATE:])
        pool_s.append(pts[:, POOL_HIST - POOL_STATE:])
        conv_s.append(zts[:, CONV_HIST - CONV_STATE:])
        v_s.append(vs.reshape(dec_batch, dec_seq, SGU_WIDTH))
    return (yp.reshape(batch, seq, D_MODEL), ys.reshape(dec_batch, dec_seq, D_MODEL),
            jnp.stack(pool_p), jnp.stack(conv_p), jnp.stack(pool_s), jnp.stack(conv_s), jnp.stack(v_s))
```

```python
import functools

import jax
import jax.numpy as jnp
from jax import lax
from jax.experimental import pallas as pl
from jax.experimental.pallas import tpu as pltpu

D_MODEL = 2048
DEPTH = 4
HEAD_DIM = 128
POOL_WINDOWS = (2, 4, 8, 16)
POOL_WIDTH = len(POOL_WINDOWS) * HEAD_DIM
N_HEADS = 6
CONV_CH = N_HEADS * HEAD_DIM
SGU_WIDTH = N_HEADS * HEAD_DIM
SGU_LEN = 128
IN_WIDTH = POOL_WIDTH + 3 * CONV_CH + 2 * SGU_WIDTH
D_FF = 5632
POOL_STATE = 15
CONV_STATE = 2
N_NORMS = 6
PAST_LEN = 1024
EPS = 1e-6

_OFF_P = 0
_OFF_H = POOL_WIDTH
_OFF_B = _OFF_H + CONV_CH
_OFF_C = _OFF_B + CONV_CH
_OFF_U = _OFF_C + CONV_CH
_OFF_V = _OFF_U + SGU_WIDTH

POOL_HIST = 16
CONV_HIST = 8

V7X_VMEM_LIMIT_BYTES = 60 * 1024 * 1024

FFN_ROWS = 1024
FFN_COLS = 512
FFN_CHUNK = 256
MIX_SUB = 256
MIX_CHAINS = 2
NORM_ROWS = 64

_F32 = jnp.float32
_BF16 = jnp.bfloat16


def _rms(xf):
    return xf * lax.rsqrt(jnp.mean(xf * xf, axis=-1, keepdims=True) + EPS)


def _ffn_step(x_ref, gpre_ref, gpost_ref, wg_ref, wu_ref, wd_ref, o_ref, xn_ref, *, first, last):
    rows = x_ref.shape[0]
    for r0 in range(0, rows, FFN_CHUNK):
        rs = slice(r0, r0 + FFN_CHUNK)
        if first:
            gpre = gpre_ref[...]
            for q0 in range(r0, r0 + FFN_CHUNK, NORM_ROWS):
                qs = slice(q0, q0 + NORM_ROWS)
                xn_ref[qs, :] = (_rms(x_ref[qs, :]) * gpre).astype(_BF16)
        xn = xn_ref[rs, :]
        gate = jnp.dot(xn, wg_ref[...], preferred_element_type=_F32)
        up = jnp.dot(xn, wu_ref[...], preferred_element_type=_F32)
        h = (gate * jax.nn.sigmoid(gate) * up).astype(_BF16)
        part = jnp.dot(h, wd_ref[...], preferred_element_type=_F32)
        o_ref[rs, :] = part if first else o_ref[rs, :] + part
        if last:
            gpost = gpost_ref[...]
            for q0 in range(r0, r0 + FFN_CHUNK, NORM_ROWS):
                qs = slice(q0, q0 + NORM_ROWS)
                o_ref[qs, :] = x_ref[qs, :] + 0.5 * (_rms(o_ref[qs, :]) * gpost)


def _ffn_kernel(*refs):
    f = pl.program_id(1)
    nf = pl.num_programs(1)
    pl.when(f == 0)(functools.partial(_ffn_step, *refs, first=True, last=False))
    pl.when(jnp.logical_and(f > 0, f < nf - 1))(functools.partial(_ffn_step, *refs, first=False, last=False))
    pl.when(f == nf - 1)(functools.partial(_ffn_step, *refs, first=False, last=True))


def _ffn(x, gains, layer, k_pre, wg, wu, wd):
    n_rows = x.shape[0]
    tm = min(FFN_ROWS, n_rows)
    grid = (n_rows // tm, D_FF // FFN_COLS)
    gain_spec = lambda k: pl.BlockSpec((None, 1, D_MODEL), lambda i, f, k=k: (layer * N_NORMS + k, 0, 0))
    return pl.pallas_call(
        _ffn_kernel,
        out_shape=jax.ShapeDtypeStruct((n_rows, D_MODEL), _F32),
        grid=grid,
        in_specs=[
            pl.BlockSpec((tm, D_MODEL), lambda i, f: (i, 0)),
            gain_spec(k_pre),
            gain_spec(k_pre + 1),
            pl.BlockSpec((None, D_MODEL, FFN_COLS), lambda i, f: (layer, 0, f)),
            pl.BlockSpec((None, D_MODEL, FFN_COLS), lambda i, f: (layer, 0, f)),
            pl.BlockSpec((None, FFN_COLS, D_MODEL), lambda i, f: (layer, f, 0)),
        ],
        out_specs=pl.BlockSpec((tm, D_MODEL), lambda i, f: (i, 0)),
        scratch_shapes=[pltpu.VMEM((tm, D_MODEL), _BF16)],
        compiler_params=pltpu.CompilerParams(
            dimension_semantics=("arbitrary", "arbitrary"),
            vmem_limit_bytes=V7X_VMEM_LIMIT_BYTES),
        name="ffn",
    )(x, gains, gains, wg, wu, wd)


def _block_diag2(a, b):
    z = jnp.zeros_like(a)
    return jnp.concatenate([jnp.concatenate([a, z], axis=1), jnp.concatenate([z, b], axis=1)], axis=0)


def _mixer_kernel(*refs, n_seq, seq_len, pos0, carried, emit_v):
    it = iter(refs)
    x_ref = next(it)
    if not carried:
        hp_ref = next(it)
        hz_ref = next(it)
    g2_ref, g3_ref, mixg_ref = next(it), next(it), next(it)
    win_ref, wout_ref, poolw_ref, pscale_ref = next(it), next(it), next(it), next(it)
    convw_ref, sguw_ref, sgub_ref = next(it), next(it), next(it)
    o_ref, ptail_ref, ztail_ref = next(it), next(it), next(it)
    if emit_v:
        v_ref = next(it)
    xn_ref, extp_ref, extz_ref, d_ref, yc_ref, ymix_ref = (next(it) for _ in range(6))
    if carried:
        carryp_ref, carryz_ref = next(it), next(it)

    S, T = n_seq, seq_len
    R = S * T
    L = min(T, SGU_LEN)
    units = R // L
    j = pl.program_id(1) if carried else 0

    if carried:
        @pl.when(j == 0)
        def _():
            carryp_ref[...] = jnp.zeros_like(carryp_ref)
            carryz_ref[...] = jnp.zeros_like(carryz_ref)

    g2, g3, mixg, pscale = g2_ref[...], g3_ref[...], mixg_ref[...], pscale_ref[...]
    w0, w1, w2 = convw_ref[0:1, :], convw_ref[1:2, :], convw_ref[2:3, :]
    pool_bd = [_block_diag2(poolw_ref[g], poolw_ref[g + 1]) for g in range(0, len(POOL_WINDOWS), 2)]
    tril = lax.broadcasted_iota(jnp.int32, (L, L), 0) >= lax.broadcasted_iota(jnp.int32, (L, L), 1)
    w_sgu = [jnp.where(tril, sguw_ref[hd, 0:L, 0:L], 0.0).astype(_BF16) for hd in range(N_HEADS)]

    for c in range(MIX_CHAINS):
        base = c * R
        rs = slice(base, base + R)
        for q0 in range(base, base + R, NORM_ROWS):
            qs = slice(q0, q0 + NORM_ROWS)
            xn_ref[qs, :] = (_rms(x_ref[qs, :]) * g2).astype(_BF16)
        proj = jnp.dot(xn_ref[rs, :], win_ref[...], preferred_element_type=_F32)
        p = proj[:, _OFF_P:_OFF_P + POOL_WIDTH]
        z = proj[:, _OFF_C:_OFF_C + CONV_CH] * proj[:, _OFF_H:_OFF_H + CONV_CH]
        bg = proj[:, _OFF_B:_OFF_B + CONV_CH]
        v = proj[:, _OFF_V:_OFF_V + SGU_WIDTH]

        for s in range(S):
            e = c * S + s
            if not carried:
                hist_p, hist_z = hp_ref[e], hz_ref[e]
            elif c == 0:
                hist_p, hist_z = carryp_ref[...], carryz_ref[...]
            else:
                hist_p, hist_z = extp_ref[e - 1, T:T + POOL_HIST, :], extz_ref[e - 1, T:T + CONV_HIST, :]
            extp_ref[e, 0:POOL_HIST, :] = hist_p
            extp_ref[e, POOL_HIST:POOL_HIST + T, :] = p[s * T:(s + 1) * T, :]
            extz_ref[e, 0:CONV_HIST, :] = hist_z
            extz_ref[e, CONV_HIST:CONV_HIST + T, :] = z[s * T:(s + 1) * T, :]

        t_idx = lax.broadcasted_iota(jnp.int32, (T, 1), 0) + (pos0 + (j * MIX_CHAINS + c) * T if carried else pos0)
        for s in range(S):
            e = c * S + s
            for g, w in enumerate(POOL_WINDOWS):
                cols = slice(g * HEAD_DIM, (g + 1) * HEAD_DIM)
                acc = extp_ref[e, POOL_HIST:POOL_HIST + T, cols]
                for k in range(1, w):
                    acc = acc + extp_ref[e, POOL_HIST - k:POOL_HIST - k + T, cols]
                cnt = jnp.minimum(t_idx + 1, w).astype(_F32)
                d_ref[base + s * T:base + (s + 1) * T, cols] = (
                    acc / cnt - extp_ref[e, POOL_HIST:POOL_HIST + T, cols]).astype(_BF16)
        ya = jnp.concatenate(
            [jnp.dot(d_ref[rs, 2 * i * HEAD_DIM:2 * (i + 1) * HEAD_DIM], pool_bd[i], preferred_element_type=_F32)
             for i in range(len(pool_bd))], axis=1) * pscale
        ymix_ref[rs, 0:POOL_WIDTH] = (_rms(ya) * mixg[:, 0:POOL_WIDTH]).astype(_BF16)

        c0 = POOL_WIDTH
        for s in range(S):
            e = c * S + s
            conv = (extz_ref[e, CONV_HIST - 2:CONV_HIST - 2 + T, :] * w0
                    + extz_ref[e, CONV_HIST - 1:CONV_HIST - 1 + T, :] * w1
                    + extz_ref[e, CONV_HIST:CONV_HIST + T, :] * w2)
            yb = bg[s * T:(s + 1) * T, :] * conv
            ymix_ref[base + s * T:base + (s + 1) * T, c0:c0 + CONV_CH] = (
                _rms(yb) * mixg[:, c0:c0 + CONV_CH]).astype(_BF16)

        if emit_v:
            v_ref[rs, :] = v
        vb = v.astype(_BF16)
        for hd in range(N_HEADS):
            cols = slice(hd * HEAD_DIM, (hd + 1) * HEAD_DIM)
            vcat = jnp.concatenate([vb[k * L:(k + 1) * L, cols] for k in range(units)], axis=1)
            gcat = jnp.dot(w_sgu[hd], vcat, preferred_element_type=_F32)
            bias = sgub_ref[0:L, cols]
            for k in range(units):
                rk = slice(base + k * L, base + (k + 1) * L)
                yc_ref[rk, cols] = (proj[k * L:(k + 1) * L, _OFF_U + hd * HEAD_DIM:_OFF_U + (hd + 1) * HEAD_DIM]
                                    * (gcat[:, k * HEAD_DIM:(k + 1) * HEAD_DIM] + bias))
        c0 = POOL_WIDTH + CONV_CH
        ymix_ref[rs, c0:c0 + SGU_WIDTH] = (_rms(yc_ref[rs, :]) * mixg[:, c0:c0 + SGU_WIDTH]).astype(_BF16)

        o_ref[rs, :] = jnp.dot(ymix_ref[rs, :], wout_ref[...], preferred_element_type=_F32)
        for q0 in range(base, base + R, NORM_ROWS):
            qs = slice(q0, q0 + NORM_ROWS)
            o_ref[qs, :] = x_ref[qs, :] + _rms(o_ref[qs, :]) * g3

    n_pieces = MIX_CHAINS * S
    if carried:
        ptail_ref[0] = extp_ref[n_pieces - 1, T:T + POOL_HIST, :]
        ztail_ref[0] = extz_ref[n_pieces - 1, T:T + CONV_HIST, :]
        carryp_ref[...] = extp_ref[n_pieces - 1, T:T + POOL_HIST, :]
        carryz_ref[...] = extz_ref[n_pieces - 1, T:T + CONV_HIST, :]
    else:
        for e in range(n_pieces):
            ptail_ref[e] = extp_ref[e, T:T + POOL_HIST, :]
            ztail_ref[e] = extz_ref[e, T:T + CONV_HIST, :]


def _mixer(x, hist_p, hist_z, layer, gains, mixg, win, wout, poolw, pscale, convw, sguw, sgub,
           *, n_batch, seq_len, pos0):
    carried = hist_p is None
    n_rows = x.shape[0]
    rows = MIX_CHAINS * MIX_SUB
    if carried:
        T, S = MIX_SUB, 1
        tiles_per_seq = seq_len // rows
        grid = (n_batch, tiles_per_seq)
        row_map = lambda b, j: (b * tiles_per_seq + j, 0)
        seq_map = lambda b, j: (b, 0, 0)
        seq_blk = 1
    else:
        T, S = seq_len, MIX_SUB // seq_len
        seq_blk = MIX_CHAINS * S
        grid = (n_batch // seq_blk,)
        row_map = lambda i: (i, 0)
        seq_map = lambda i: (i, 0, 0)
    nd = len(grid)
    n_pieces = MIX_CHAINS * S

    def const_spec(shape, idx):
        return pl.BlockSpec(shape, lambda *_: idx, pipeline_mode=pl.Buffered(1))

    in_specs = [pl.BlockSpec((rows, D_MODEL), row_map)]
    args = [x]
    if not carried:
        in_specs += [pl.BlockSpec((seq_blk, POOL_HIST, POOL_WIDTH), seq_map),
                     pl.BlockSpec((seq_blk, CONV_HIST, CONV_CH), seq_map)]
        args += [hist_p, hist_z]
    in_specs += [
        const_spec((None, 1, D_MODEL), (layer * N_NORMS + 2, 0, 0)),
        const_spec((None, 1, D_MODEL), (layer * N_NORMS + 3, 0, 0)),
        const_spec((None, 1, D_MODEL), (layer, 0, 0)),
        const_spec((None, D_MODEL, IN_WIDTH), (layer, 0, 0)),
        const_spec((None, D_MODEL, D_MODEL), (layer, 0, 0)),
        const_spec((None, len(POOL_WINDOWS), HEAD_DIM, HEAD_DIM), (layer, 0, 0, 0)),
        const_spec((None, 1, POOL_WIDTH), (layer, 0, 0)),
        const_spec((None, CONV_HIST, CONV_CH), (layer, 0, 0)),
        const_spec((None, N_HEADS, SGU_LEN, SGU_LEN), (layer, 0, 0, 0)),
        const_spec((None, SGU_LEN, SGU_WIDTH), (layer, 0, 0)),
    ]
    args += [gains, gains, mixg, win, wout, poolw, pscale, convw, sguw, sgub]

    out_shape = [jax.ShapeDtypeStruct((n_rows, D_MODEL), _F32),
                 jax.ShapeDtypeStruct((n_batch, POOL_HIST, POOL_WIDTH), _F32),
                 jax.ShapeDtypeStruct((n_batch, CONV_HIST, CONV_CH), _F32)]
    out_specs = [pl.BlockSpec((rows, D_MODEL), row_map),
                 pl.BlockSpec((seq_blk, POOL_HIST, POOL_WIDTH), seq_map),
                 pl.BlockSpec((seq_blk, CONV_HIST, CONV_CH), seq_map)]
    emit_v = not carried
    if emit_v:
        out_shape.append(jax.ShapeDtypeStruct((n_rows, SGU_WIDTH), _F32))
        out_specs.append(pl.BlockSpec((rows, SGU_WIDTH), row_map))

    scratch = [
        pltpu.VMEM((rows, D_MODEL), _BF16),
        pltpu.VMEM((n_pieces, POOL_HIST + T, POOL_WIDTH), _F32),
        pltpu.VMEM((n_pieces, CONV_HIST + T, CONV_CH), _F32),
        pltpu.VMEM((rows, POOL_WIDTH), _BF16),
        pltpu.VMEM((rows, SGU_WIDTH), _F32),
        pltpu.VMEM((rows, D_MODEL), _BF16),
    ]
    if carried:
        scratch += [pltpu.VMEM((POOL_HIST, POOL_WIDTH), _F32), pltpu.VMEM((CONV_HIST, CONV_CH), _F32)]

    kern = functools.partial(_mixer_kernel, n_seq=S, seq_len=T, pos0=pos0, carried=carried, emit_v=emit_v)
    return pl.pallas_call(
        kern,
        out_shape=out_shape,
        grid=grid,
        in_specs=in_specs,
        out_specs=out_specs,
        scratch_shapes=scratch,
        compiler_params=pltpu.CompilerParams(
            dimension_semantics=("arbitrary",) * nd,
            vmem_limit_bytes=V7X_VMEM_LIMIT_BYTES),
        name="mixer_prompt" if carried else "mixer_sample",
    )(*args)


def kernel(x_prompt, x_sample, state_pool, state_conv, w_in, w_out, pool_w, pool_scale, conv_w, sgu_w,
           sgu_b, ffn1_gate, ffn1_up, ffn1_down, ffn2_gate, ffn2_up, ffn2_down, norm_gains, mix_gain):
    batch, seq, _ = x_prompt.shape
    dec_batch, dec_seq, _ = x_sample.shape

    bf = lambda w: w.astype(_BF16)
    win_b, wout_b, poolw_b = bf(w_in), bf(w_out), bf(pool_w)
    f1g, f1u, f1d = bf(ffn1_gate), bf(ffn1_up), bf(ffn1_down)
    f2g, f2u, f2d = bf(ffn2_gate), bf(ffn2_up), bf(ffn2_down)
    gains = norm_gains.reshape(DEPTH * N_NORMS, 1, D_MODEL)
    mixg = mix_gain.reshape(DEPTH, 1, D_MODEL)
    pscale = pool_scale.reshape(DEPTH, 1, POOL_WIDTH)
    convw = jnp.pad(conv_w, ((0, 0), (0, CONV_HIST - conv_w.shape[1]), (0, 0)))
    sgub = jnp.repeat(jnp.swapaxes(sgu_b, 1, 2), HEAD_DIM, axis=2)
    hist_p = jnp.pad(state_pool, ((0, 0), (0, 0), (POOL_HIST - POOL_STATE, 0), (0, 0)))
    hist_z = jnp.pad(state_conv, ((0, 0), (0, 0), (CONV_HIST - CONV_STATE, 0), (0, 0)))

    yp = x_prompt.reshape(batch * seq, D_MODEL)
    ys = x_sample.reshape(dec_batch * dec_seq, D_MODEL)
    pool_p, conv_p, pool_s, conv_s, v_s = [], [], [], [], []
    for l in range(DEPTH):
        mix_w = (gains, mixg, win_b, wout_b, poolw_b, pscale, convw, sgu_w, sgub)
        yp = _ffn(yp, gains, l, 0, f1g, f1u, f1d)
        ys = _ffn(ys, gains, l, 0, f1g, f1u, f1d)
        yp, pt, zt = _mixer(yp, None, None, l, *mix_w, n_batch=batch, seq_len=seq, pos0=0)
        ys, pts, zts, vs = _mixer(ys, hist_p[l], hist_z[l], l, *mix_w,
                                  n_batch=dec_batch, seq_len=dec_seq, pos0=PAST_LEN)
        yp = _ffn(yp, gains, l, 4, f2g, f2u, f2d)
        ys = _ffn(ys, gains, l, 4, f2g, f2u, f2d)
        pool_p.append(pt[:, POOL_HIST - POOL_STATE:])
        conv_p.append(zt[:, CONV_HIST - CONV_STATE:])
        pool_s.append(pts[:, POOL_HIST - POOL_STATE:])
        conv_s.append(zts[:, CONV_HIST - CONV_STATE:])
        v_s.append(vs.reshape(dec_batch, dec_seq, SGU_WIDTH))
    return (yp.reshape(batch, seq, D_MODEL), ys.reshape(dec_batch, dec_seq, D_MODEL),
            jnp.stack(pool_p), jnp.stack(conv_p), jnp.stack(pool_s), jnp.stack(conv_s), jnp.stack(v_s))
```

```python
import functools

import jax
import jax.numpy as jnp
from jax import lax
from jax.experimental import pallas as pl
from jax.experimental.pallas import tpu as pltpu

D_MODEL = 2048
DEPTH = 4
HEAD_DIM = 128
POOL_WINDOWS = (2, 4, 8, 16)
POOL_WIDTH = len(POOL_WINDOWS) * HEAD_DIM
N_HEADS = 6
CONV_CH = N_HEADS * HEAD_DIM
SGU_WIDTH = N_HEADS * HEAD_DIM
SGU_LEN = 128
IN_WIDTH = POOL_WIDTH + 3 * CONV_CH + 2 * SGU_WIDTH
D_FF = 5632
POOL_STATE = 15
CONV_STATE = 2
N_NORMS = 6
PAST_LEN = 1024
EPS = 1e-6

_OFF_P = 0
_OFF_H = POOL_WIDTH
_OFF_B = _OFF_H + CONV_CH
_OFF_C = _OFF_B + CONV_CH
_OFF_U = _OFF_C + CONV_CH
_OFF_V = _OFF_U + SGU_WIDTH

POOL_HIST = 16
CONV_HIST = 8

V7X_VMEM_LIMIT_BYTES = 60 * 1024 * 1024
BF16_SUBLANES = 16

FFN_ROWS = 1024
FFN_COLS = 512
FFN_CHUNK = 256
MIX_SUB = 256
MIX_CHAINS = 2
NORM_ROWS = 64

_F32 = jnp.float32
_BF16 = jnp.bfloat16


def _rms(xf):
    return xf * lax.rsqrt(jnp.mean(xf * xf, axis=-1, keepdims=True) + EPS)


def _ffn_step(*refs, n_cast, first, last):
    x_ref, gpre_ref, gpost_ref, wg_ref, wu_ref, wd_ref = refs[:6]
    cast_src = refs[6:6 + n_cast]
    o_ref = refs[6 + n_cast]
    cast_dst = refs[7 + n_cast:7 + 2 * n_cast]
    xn_ref = refs[7 + 2 * n_cast]

    for src, dst in zip(cast_src, cast_dst):
        dst[...] = src[...].astype(_BF16)

    rows = x_ref.shape[0]
    n_chunks = rows // FFN_CHUNK

    def up_phase(c):
        r0 = c * FFN_CHUNK
        if first:
            gpre = gpre_ref[...]
            for q0 in range(r0, r0 + FFN_CHUNK, NORM_ROWS):
                qs = slice(q0, q0 + NORM_ROWS)
                xn_ref[qs, :] = (_rms(x_ref[qs, :]) * gpre).astype(_BF16)
        xn = xn_ref[r0:r0 + FFN_CHUNK, :]
        gate = jnp.dot(xn, wg_ref[...], preferred_element_type=_F32)
        up = jnp.dot(xn, wu_ref[...], preferred_element_type=_F32)
        return (gate * jax.nn.sigmoid(gate) * up).astype(_BF16)

    def down_phase(c, h):
        r0 = c * FFN_CHUNK
        rs = slice(r0, r0 + FFN_CHUNK)
        part = jnp.dot(h, wd_ref[...], preferred_element_type=_F32)
        o_ref[rs, :] = part if first else o_ref[rs, :] + part
        if last:
            gpost = gpost_ref[...]
            for q0 in range(r0, r0 + FFN_CHUNK, NORM_ROWS):
                qs = slice(q0, q0 + NORM_ROWS)
                o_ref[qs, :] = x_ref[qs, :] + 0.5 * (_rms(o_ref[qs, :]) * gpost)

    h_prev = up_phase(0)
    for c in range(1, n_chunks):
        h_next = up_phase(c)
        down_phase(c - 1, h_prev)
        h_prev = h_next
    down_phase(n_chunks - 1, h_prev)


def _ffn_kernel(*refs, n_cast):
    f = pl.program_id(1)
    nf = pl.num_programs(1)
    step = functools.partial(_ffn_step, *refs, n_cast=n_cast)
    pl.when(f == 0)(functools.partial(step, first=True, last=False))
    pl.when(jnp.logical_and(f > 0, f < nf - 1))(functools.partial(step, first=False, last=False))
    pl.when(f == nf - 1)(functools.partial(step, first=False, last=True))


def _cast_rows_per_step(n_rows, n_steps):
    r = BF16_SUBLANES
    while n_rows % r or n_rows // r > n_steps:
        r += BF16_SUBLANES
    return r


def _ffn(x, gains, layer, k_pre, wg, wu, wd, casts=()):
    n_rows = x.shape[0]
    tm = min(FFN_ROWS, n_rows)
    n_f = D_FF // FFN_COLS
    grid = (n_rows // tm, n_f)
    n_steps = grid[0] * grid[1]
    gain_spec = lambda k: pl.BlockSpec((None, 1, D_MODEL), lambda i, f, k=k: (layer * N_NORMS + k, 0, 0))
    in_specs = [
        pl.BlockSpec((tm, D_MODEL), lambda i, f: (i, 0)),
        gain_spec(k_pre),
        gain_spec(k_pre + 1),
        pl.BlockSpec((D_MODEL, FFN_COLS), lambda i, f: (0, f)),
        pl.BlockSpec((D_MODEL, FFN_COLS), lambda i, f: (0, f)),
        pl.BlockSpec((FFN_COLS, D_MODEL), lambda i, f: (f, 0)),
    ]
    out_shape = [jax.ShapeDtypeStruct((n_rows, D_MODEL), _F32)]
    out_specs = [pl.BlockSpec((tm, D_MODEL), lambda i, f: (i, 0))]
    for w, wl in casts:
        _, r_total, cols = w.shape
        r = _cast_rows_per_step(r_total, n_steps)
        last_blk = r_total // r - 1
        in_specs.append(pl.BlockSpec(
            (None, r, cols), lambda i, f, wl=wl, last_blk=last_blk: (wl, jnp.minimum(i * n_f + f, last_blk), 0)))
        out_shape.append(jax.ShapeDtypeStruct((r_total, cols), _BF16))
        out_specs.append(pl.BlockSpec(
            (r, cols), lambda i, f, last_blk=last_blk: (jnp.minimum(i * n_f + f, last_blk), 0)))
    res = pl.pallas_call(
        functools.partial(_ffn_kernel, n_cast=len(casts)),
        out_shape=out_shape,
        grid=grid,
        in_specs=in_specs,
        out_specs=out_specs,
        scratch_shapes=[pltpu.VMEM((tm, D_MODEL), _BF16)],
        compiler_params=pltpu.CompilerParams(
            dimension_semantics=("arbitrary", "arbitrary"),
            vmem_limit_bytes=V7X_VMEM_LIMIT_BYTES),
        name="ffn",
    )(x, gains, gains, wg, wu, wd, *[w for w, _ in casts])
    return res[0] if not casts else res


def _block_diag2(a, b):
    z = jnp.zeros_like(a)
    return jnp.concatenate([jnp.concatenate([a, z], axis=1), jnp.concatenate([z, b], axis=1)], axis=0)


def _mixer_kernel(*refs, n_seq, seq_len, pos0, carried, emit_v):
    it = iter(refs)
    x_ref = next(it)
    if not carried:
        hp_ref = next(it)
        hz_ref = next(it)
    g2_ref, g3_ref, mixg_ref = next(it), next(it), next(it)
    win_ref, wout_ref, poolw_ref, pscale_ref = next(it), next(it), next(it), next(it)
    convw_ref, sguw_ref, sgub_ref = next(it), next(it), next(it)
    o_ref, ptail_ref, ztail_ref = next(it), next(it), next(it)
    if emit_v:
        v_ref = next(it)
    xn_ref, extp_ref, extz_ref, d_ref, yc_ref, ymix_ref = (next(it) for _ in range(6))
    if carried:
        carryp_ref, carryz_ref = next(it), next(it)

    S, T = n_seq, seq_len
    R = S * T
    L = min(T, SGU_LEN)
    units = R // L
    j = pl.program_id(1) if carried else 0

    if carried:
        @pl.when(j == 0)
        def _():
            carryp_ref[...] = jnp.zeros_like(carryp_ref)
            carryz_ref[...] = jnp.zeros_like(carryz_ref)

    g2, g3, mixg, pscale = g2_ref[...], g3_ref[...], mixg_ref[...], pscale_ref[...]
    w0, w1, w2 = convw_ref[0:1, :], convw_ref[1:2, :], convw_ref[2:3, :]
    pool_bd = [_block_diag2(poolw_ref[g], poolw_ref[g + 1]) for g in range(0, len(POOL_WINDOWS), 2)]
    tril = lax.broadcasted_iota(jnp.int32, (L, L), 0) >= lax.broadcasted_iota(jnp.int32, (L, L), 1)
    w_sgu = [jnp.where(tril, sguw_ref[hd, 0:L, 0:L], 0.0).astype(_BF16) for hd in range(N_HEADS)]

    for c in range(MIX_CHAINS):
        base = c * R
        rs = slice(base, base + R)
        for q0 in range(base, base + R, NORM_ROWS):
            qs = slice(q0, q0 + NORM_ROWS)
            xn_ref[qs, :] = (_rms(x_ref[qs, :]) * g2).astype(_BF16)
        proj = jnp.dot(xn_ref[rs, :], win_ref[...], preferred_element_type=_F32)
        p = proj[:, _OFF_P:_OFF_P + POOL_WIDTH]
        z = proj[:, _OFF_C:_OFF_C + CONV_CH] * proj[:, _OFF_H:_OFF_H + CONV_CH]
        bg = proj[:, _OFF_B:_OFF_B + CONV_CH]
        v = proj[:, _OFF_V:_OFF_V + SGU_WIDTH]

        for s in range(S):
            e = c * S + s
            if not carried:
                hist_p, hist_z = hp_ref[e], hz_ref[e]
            elif c == 0:
                hist_p, hist_z = carryp_ref[...], carryz_ref[...]
            else:
                hist_p, hist_z = extp_ref[e - 1, T:T + POOL_HIST, :], extz_ref[e - 1, T:T + CONV_HIST, :]
            extp_ref[e, 0:POOL_HIST, :] = hist_p
            extp_ref[e, POOL_HIST:POOL_HIST + T, :] = p[s * T:(s + 1) * T, :]
            extz_ref[e, 0:CONV_HIST, :] = hist_z
            extz_ref[e, CONV_HIST:CONV_HIST + T, :] = z[s * T:(s + 1) * T, :]

        t_idx = lax.broadcasted_iota(jnp.int32, (T, 1), 0) + (pos0 + (j * MIX_CHAINS + c) * T if carried else pos0)
        for s in range(S):
            e = c * S + s
            for g, w in enumerate(POOL_WINDOWS):
                cols = slice(g * HEAD_DIM, (g + 1) * HEAD_DIM)
                acc = extp_ref[e, POOL_HIST:POOL_HIST + T, cols]
                for k in range(1, w):
                    acc = acc + extp_ref[e, POOL_HIST - k:POOL_HIST - k + T, cols]
                cnt = jnp.minimum(t_idx + 1, w).astype(_F32)
                d_ref[base + s * T:base + (s + 1) * T, cols] = (
                    acc / cnt - extp_ref[e, POOL_HIST:POOL_HIST + T, cols]).astype(_BF16)
        ya = jnp.concatenate(
            [jnp.dot(d_ref[rs, 2 * i * HEAD_DIM:2 * (i + 1) * HEAD_DIM], pool_bd[i], preferred_element_type=_F32)
             for i in range(len(pool_bd))], axis=1) * pscale
        ymix_ref[rs, 0:POOL_WIDTH] = (_rms(ya) * mixg[:, 0:POOL_WIDTH]).astype(_BF16)

        c0 = POOL_WIDTH
        for s in range(S):
            e = c * S + s
            conv = (extz_ref[e, CONV_HIST - 2:CONV_HIST - 2 + T, :] * w0
                    + extz_ref[e, CONV_HIST - 1:CONV_HIST - 1 + T, :] * w1
                    + extz_ref[e, CONV_HIST:CONV_HIST + T, :] * w2)
            yb = bg[s * T:(s + 1) * T, :] * conv
            ymix_ref[base + s * T:base + (s + 1) * T, c0:c0 + CONV_CH] = (
                _rms(yb) * mixg[:, c0:c0 + CONV_CH]).astype(_BF16)

        if emit_v:
            v_ref[rs, :] = v
        vb = v.astype(_BF16)
        for hd in range(N_HEADS):
            cols = slice(hd * HEAD_DIM, (hd + 1) * HEAD_DIM)
            vcat = jnp.concatenate([vb[k * L:(k + 1) * L, cols] for k in range(units)], axis=1)
            gcat = jnp.dot(w_sgu[hd], vcat, preferred_element_type=_F32)
            bias = sgub_ref[0:L, cols]
            for k in range(units):
                rk = slice(base + k * L, base + (k + 1) * L)
                yc_ref[rk, cols] = (proj[k * L:(k + 1) * L, _OFF_U + hd * HEAD_DIM:_OFF_U + (hd + 1) * HEAD_DIM]
                                    * (gcat[:, k * HEAD_DIM:(k + 1) * HEAD_DIM] + bias))
        c0 = POOL_WIDTH + CONV_CH
        ymix_ref[rs, c0:c0 + SGU_WIDTH] = (_rms(yc_ref[rs, :]) * mixg[:, c0:c0 + SGU_WIDTH]).astype(_BF16)

        o_ref[rs, :] = jnp.dot(ymix_ref[rs, :], wout_ref[...], preferred_element_type=_F32)
        for q0 in range(base, base + R, NORM_ROWS):
            qs = slice(q0, q0 + NORM_ROWS)
            o_ref[qs, :] = x_ref[qs, :] + _rms(o_ref[qs, :]) * g3

    n_pieces = MIX_CHAINS * S
    if carried:
        ptail_ref[0] = extp_ref[n_pieces - 1, T:T + POOL_HIST, :]
        ztail_ref[0] = extz_ref[n_pieces - 1, T:T + CONV_HIST, :]
        carryp_ref[...] = extp_ref[n_pieces - 1, T:T + POOL_HIST, :]
        carryz_ref[...] = extz_ref[n_pieces - 1, T:T + CONV_HIST, :]
    else:
        for e in range(n_pieces):
            ptail_ref[e] = extp_ref[e, T:T + POOL_HIST, :]
            ztail_ref[e] = extz_ref[e, T:T + CONV_HIST, :]


def _mixer(x, hist_p, hist_z, layer, gains, mixg, win, wout, poolw, pscale, convw, sguw, sgub,
           *, n_batch, seq_len, pos0):
    carried = hist_p is None
    n_rows = x.shape[0]
    rows = MIX_CHAINS * MIX_SUB
    if carried:
        T, S = MIX_SUB, 1
        tiles_per_seq = seq_len // rows
        grid = (n_batch, tiles_per_seq)
        row_map = lambda b, j: (b * tiles_per_seq + j, 0)
        seq_map = lambda b, j: (b, 0, 0)
        seq_blk = 1
    else:
        T, S = seq_len, MIX_SUB // seq_len
        seq_blk = MIX_CHAINS * S
        grid = (n_batch // seq_blk,)
        row_map = lambda i: (i, 0)
        seq_map = lambda i: (i, 0, 0)
    nd = len(grid)
    n_pieces = MIX_CHAINS * S

    def const_spec(shape, idx):
        return pl.BlockSpec(shape, lambda *_: idx, pipeline_mode=pl.Buffered(1))

    in_specs = [pl.BlockSpec((rows, D_MODEL), row_map)]
    args = [x]
    if not carried:
        in_specs += [pl.BlockSpec((seq_blk, POOL_HIST, POOL_WIDTH), seq_map),
                     pl.BlockSpec((seq_blk, CONV_HIST, CONV_CH), seq_map)]
        args += [hist_p, hist_z]
    in_specs += [
        const_spec((None, 1, D_MODEL), (layer * N_NORMS + 2, 0, 0)),
        const_spec((None, 1, D_MODEL), (layer * N_NORMS + 3, 0, 0)),
        const_spec((None, 1, D_MODEL), (layer, 0, 0)),
        const_spec((D_MODEL, IN_WIDTH), (0, 0)),
        const_spec((D_MODEL, D_MODEL), (0, 0)),
        const_spec((None, len(POOL_WINDOWS), HEAD_DIM, HEAD_DIM), (layer, 0, 0, 0)),
        const_spec((None, 1, POOL_WIDTH), (layer, 0, 0)),
        const_spec((None, CONV_HIST, CONV_CH), (layer, 0, 0)),
        const_spec((None, N_HEADS, SGU_LEN, SGU_LEN), (layer, 0, 0, 0)),
        const_spec((None, SGU_LEN, SGU_WIDTH), (layer, 0, 0)),
    ]
    args += [gains, gains, mixg, win, wout, poolw, pscale, convw, sguw, sgub]

    out_shape = [jax.ShapeDtypeStruct((n_rows, D_MODEL), _F32),
                 jax.ShapeDtypeStruct((n_batch, POOL_HIST, POOL_WIDTH), _F32),
                 jax.ShapeDtypeStruct((n_batch, CONV_HIST, CONV_CH), _F32)]
    out_specs = [pl.BlockSpec((rows, D_MODEL), row_map),
                 pl.BlockSpec((seq_blk, POOL_HIST, POOL_WIDTH), seq_map),
                 pl.BlockSpec((seq_blk, CONV_HIST, CONV_CH), seq_map)]
    emit_v = not carried
    if emit_v:
        out_shape.append(jax.ShapeDtypeStruct((n_rows, SGU_WIDTH), _F32))
        out_specs.append(pl.BlockSpec((rows, SGU_WIDTH), row_map))

    scratch = [
        pltpu.VMEM((rows, D_MODEL), _BF16),
        pltpu.VMEM((n_pieces, POOL_HIST + T, POOL_WIDTH), _F32),
        pltpu.VMEM((n_pieces, CONV_HIST + T, CONV_CH), _F32),
        pltpu.VMEM((rows, POOL_WIDTH), _BF16),
        pltpu.VMEM((rows, SGU_WIDTH), _F32),
        pltpu.VMEM((rows, D_MODEL), _BF16),
    ]
    if carried:
        scratch += [pltpu.VMEM((POOL_HIST, POOL_WIDTH), _F32), pltpu.VMEM((CONV_HIST, CONV_CH), _F32)]

    kern = functools.partial(_mixer_kernel, n_seq=S, seq_len=T, pos0=pos0, carried=carried, emit_v=emit_v)
    return pl.pallas_call(
        kern,
        out_shape=out_shape,
        grid=grid,
        in_specs=in_specs,
        out_specs=out_specs,
        scratch_shapes=scratch,
        compiler_params=pltpu.CompilerParams(
            dimension_semantics=("arbitrary",) * nd,
            vmem_limit_bytes=V7X_VMEM_LIMIT_BYTES),
        name="mixer_prompt" if carried else "mixer_sample",
    )(*args)


def kernel(x_prompt, x_sample, state_pool, state_conv, w_in, w_out, pool_w, pool_scale, conv_w, sgu_w,
           sgu_b, ffn1_gate, ffn1_up, ffn1_down, ffn2_gate, ffn2_up, ffn2_down, norm_gains, mix_gain):
    batch, seq, _ = x_prompt.shape
    dec_batch, dec_seq, _ = x_sample.shape

    poolw_b = pool_w.astype(_BF16)
    f1 = tuple(w[0].astype(_BF16) for w in (ffn1_gate, ffn1_up, ffn1_down))
    gains = norm_gains.reshape(DEPTH * N_NORMS, 1, D_MODEL)
    mixg = mix_gain.reshape(DEPTH, 1, D_MODEL)
    pscale = pool_scale.reshape(DEPTH, 1, POOL_WIDTH)
    convw = jnp.pad(conv_w, ((0, 0), (0, CONV_HIST - conv_w.shape[1]), (0, 0)))
    sgub = jnp.repeat(jnp.swapaxes(sgu_b, 1, 2), HEAD_DIM, axis=2)
    hist_p = jnp.pad(state_pool, ((0, 0), (0, 0), (POOL_HIST - POOL_STATE, 0), (0, 0)))
    hist_z = jnp.pad(state_conv, ((0, 0), (0, 0), (CONV_HIST - CONV_STATE, 0), (0, 0)))

    yp = x_prompt.reshape(batch * seq, D_MODEL)
    ys = x_sample.reshape(dec_batch * dec_seq, D_MODEL)
    pool_p, conv_p, pool_s, conv_s, v_s = [], [], [], [], []
    for l in range(DEPTH):
        yp, win_b, wout_b, *f2 = _ffn(
            yp, gains, l, 0, *f1,
            casts=[(w_in, l), (w_out, l), (ffn2_gate, l), (ffn2_up, l), (ffn2_down, l)])
        ys = _ffn(ys, gains, l, 0, *f1)
        mix_w = (gains, mixg, win_b, wout_b, poolw_b, pscale, convw, sgu_w, sgub)
        yp, pt, zt = _mixer(yp, None, None, l, *mix_w, n_batch=batch, seq_len=seq, pos0=0)
        ys, pts, zts, vs = _mixer(ys, hist_p[l], hist_z[l], l, *mix_w,
                                  n_batch=dec_batch, seq_len=dec_seq, pos0=PAST_LEN)
        if l + 1 < DEPTH:
            yp, *f1_next = _ffn(
                yp, gains, l, 4, *f2,
                casts=[(ffn1_gate, l + 1), (ffn1_up, l + 1), (ffn1_down, l + 1)])
        else:
            yp, f1_next = _ffn(yp, gains, l, 4, *f2), None
        ys = _ffn(ys, gains, l, 4, *f2)
        f1 = f1_next
        pool_p.append(pt[:, POOL_HIST - POOL_STATE:])
        conv_p.append(zt[:, CONV_HIST - CONV_STATE:])
        pool_s.append(pts[:, POOL_HIST - POOL_STATE:])
        conv_s.append(zts[:, CONV_HIST - CONV_STATE:])
        v_s.append(vs.reshape(dec_batch, dec_seq, SGU_WIDTH))
    return (yp.reshape(batch, seq, D_MODEL), ys.reshape(dec_batch, dec_seq, D_MODEL),
            jnp.stack(pool_p), jnp.stack(conv_p), jnp.stack(pool_s), jnp.stack(conv_s), jnp.stack(v_s))
```

```python
import functools

import jax
import jax.numpy as jnp
from jax import lax
from jax.experimental import pallas as pl
from jax.experimental.pallas import tpu as pltpu

D_MODEL = 2048
DEPTH = 4
HEAD_DIM = 128
POOL_WINDOWS = (2, 4, 8, 16)
POOL_WIDTH = len(POOL_WINDOWS) * HEAD_DIM
N_HEADS = 6
CONV_CH = N_HEADS * HEAD_DIM
SGU_WIDTH = N_HEADS * HEAD_DIM
SGU_LEN = 128
IN_WIDTH = POOL_WIDTH + 3 * CONV_CH + 2 * SGU_WIDTH
D_FF = 5632
POOL_STATE = 15
CONV_STATE = 2
N_NORMS = 6
PAST_LEN = 1024
EPS = 1e-6

_OFF_P = 0
_OFF_H = POOL_WIDTH
_OFF_B = _OFF_H + CONV_CH
_OFF_C = _OFF_B + CONV_CH
_OFF_U = _OFF_C + CONV_CH
_OFF_V = _OFF_U + SGU_WIDTH

POOL_HIST = 16
CONV_HIST = 8

V7X_VMEM_LIMIT_BYTES = 60 * 1024 * 1024
BF16_SUBLANES = 16

FFN_ROWS = 1024
FFN_COLS = 512
FFN_CHUNK = 256
MIX_SUB = 256
MIX_CHAINS = 2
NORM_ROWS = 64

_F32 = jnp.float32
_BF16 = jnp.bfloat16


def _rms(xf):
    return xf * lax.rsqrt(jnp.mean(xf * xf, axis=-1, keepdims=True) + EPS)


def _ffn_step(*refs, n_cast, first, last):
    x_ref, gpre_ref, gpost_ref, wg_ref, wu_ref, wd_ref = refs[:6]
    cast_src = refs[6:6 + n_cast]
    o_ref = refs[6 + n_cast]
    cast_dst = refs[7 + n_cast:7 + 2 * n_cast]
    xn_ref = refs[7 + 2 * n_cast]

    for src, dst in zip(cast_src, cast_dst):
        if len(dst.shape) == 2:
            dst[...] = src[...].astype(_BF16)
        else:
            width = dst.shape[2]
            for k in range(dst.shape[0]):
                dst[k] = src[:, k * width:(k + 1) * width].astype(_BF16)

    rows = x_ref.shape[0]
    n_chunks = rows // FFN_CHUNK

    def up_phase(c):
        r0 = c * FFN_CHUNK
        if first:
            gpre = gpre_ref[...]
            for q0 in range(r0, r0 + FFN_CHUNK, NORM_ROWS):
                qs = slice(q0, q0 + NORM_ROWS)
                xn_ref[qs, :] = (_rms(x_ref[qs, :]) * gpre).astype(_BF16)
        xn = xn_ref[r0:r0 + FFN_CHUNK, :]
        gate = jnp.dot(xn, wg_ref[...], preferred_element_type=_F32)
        up = jnp.dot(xn, wu_ref[...], preferred_element_type=_F32)
        return (gate * jax.nn.sigmoid(gate) * up).astype(_BF16)

    def down_phase(c, h):
        r0 = c * FFN_CHUNK
        rs = slice(r0, r0 + FFN_CHUNK)
        part = jnp.dot(h, wd_ref[...], preferred_element_type=_F32)
        o_ref[rs, :] = part if first else o_ref[rs, :] + part
        if last:
            gpost = gpost_ref[...]
            for q0 in range(r0, r0 + FFN_CHUNK, NORM_ROWS):
                qs = slice(q0, q0 + NORM_ROWS)
                o_ref[qs, :] = x_ref[qs, :] + 0.5 * (_rms(o_ref[qs, :]) * gpost)

    h_prev = up_phase(0)
    for c in range(1, n_chunks):
        h_next = up_phase(c)
        down_phase(c - 1, h_prev)
        h_prev = h_next
    down_phase(n_chunks - 1, h_prev)


def _ffn_kernel(*refs, n_cast):
    f = pl.program_id(1)
    nf = pl.num_programs(1)
    step = functools.partial(_ffn_step, *refs, n_cast=n_cast)
    pl.when(f == 0)(functools.partial(step, first=True, last=False))
    pl.when(jnp.logical_and(f > 0, f < nf - 1))(functools.partial(step, first=False, last=False))
    pl.when(f == nf - 1)(functools.partial(step, first=False, last=True))


def _cast_rows_per_step(n_rows, n_steps):
    r = BF16_SUBLANES
    while n_rows % r or n_rows // r > n_steps:
        r += BF16_SUBLANES
    return r


def _ffn(x, gains, layer, k_pre, wg, wu, wd, casts=()):
    n_rows = x.shape[0]
    tm = min(FFN_ROWS, n_rows)
    n_f = D_FF // FFN_COLS
    grid = (n_rows // tm, n_f)
    n_steps = grid[0] * grid[1]
    gain_spec = lambda k: pl.BlockSpec((None, 1, D_MODEL), lambda i, f, k=k: (layer * N_NORMS + k, 0, 0))
    in_specs = [
        pl.BlockSpec((tm, D_MODEL), lambda i, f: (i, 0)),
        gain_spec(k_pre),
        gain_spec(k_pre + 1),
        pl.BlockSpec((None, D_MODEL, FFN_COLS), lambda i, f: (f, 0, 0)),
        pl.BlockSpec((None, D_MODEL, FFN_COLS), lambda i, f: (f, 0, 0)),
        pl.BlockSpec((FFN_COLS, D_MODEL), lambda i, f: (f, 0)),
    ]
    out_shape = [jax.ShapeDtypeStruct((n_rows, D_MODEL), _F32)]
    out_specs = [pl.BlockSpec((tm, D_MODEL), lambda i, f: (i, 0))]
    for w, wl, chunk in casts:
        _, r_total, cols = w.shape
        r = _cast_rows_per_step(r_total, n_steps)
        last_blk = r_total // r - 1
        in_specs.append(pl.BlockSpec(
            (None, r, cols), lambda i, f, wl=wl, last_blk=last_blk: (wl, jnp.minimum(i * n_f + f, last_blk), 0)))
        if chunk is None:
            out_shape.append(jax.ShapeDtypeStruct((r_total, cols), _BF16))
            out_specs.append(pl.BlockSpec(
                (r, cols), lambda i, f, last_blk=last_blk: (jnp.minimum(i * n_f + f, last_blk), 0)))
        else:
            out_shape.append(jax.ShapeDtypeStruct((cols // chunk, r_total, chunk), _BF16))
            out_specs.append(pl.BlockSpec(
                (cols // chunk, r, chunk),
                lambda i, f, last_blk=last_blk: (0, jnp.minimum(i * n_f + f, last_blk), 0)))
    res = pl.pallas_call(
        functools.partial(_ffn_kernel, n_cast=len(casts)),
        out_shape=out_shape,
        grid=grid,
        in_specs=in_specs,
        out_specs=out_specs,
        scratch_shapes=[pltpu.VMEM((tm, D_MODEL), _BF16)],
        compiler_params=pltpu.CompilerParams(
            dimension_semantics=("arbitrary", "arbitrary"),
            vmem_limit_bytes=V7X_VMEM_LIMIT_BYTES),
        name="ffn",
    )(x, gains, gains, wg, wu, wd, *[c[0] for c in casts])
    return res[0] if not casts else res


def _block_diag2(a, b):
    z = jnp.zeros_like(a)
    return jnp.concatenate([jnp.concatenate([a, z], axis=1), jnp.concatenate([z, b], axis=1)], axis=0)


def _mixer_kernel(*refs, n_seq, seq_len, pos0, carried, emit_v):
    it = iter(refs)
    x_ref = next(it)
    if not carried:
        hp_ref = next(it)
        hz_ref = next(it)
    g2_ref, g3_ref, mixg_ref = next(it), next(it), next(it)
    win_ref, wout_ref, poolw_ref, pscale_ref = next(it), next(it), next(it), next(it)
    convw_ref, sguw_ref, sgub_ref = next(it), next(it), next(it)
    o_ref, ptail_ref, ztail_ref = next(it), next(it), next(it)
    if emit_v:
        v_ref = next(it)
    xn_ref, extp_ref, extz_ref, d_ref, yc_ref, ymix_ref = (next(it) for _ in range(6))
    if carried:
        carryp_ref, carryz_ref = next(it), next(it)

    S, T = n_seq, seq_len
    R = S * T
    L = min(T, SGU_LEN)
    units = R // L
    j = pl.program_id(1) if carried else 0

    if carried:
        @pl.when(j == 0)
        def _():
            carryp_ref[...] = jnp.zeros_like(carryp_ref)
            carryz_ref[...] = jnp.zeros_like(carryz_ref)

    g2, g3, mixg, pscale = g2_ref[...], g3_ref[...], mixg_ref[...], pscale_ref[...]
    w0, w1, w2 = convw_ref[0:1, :], convw_ref[1:2, :], convw_ref[2:3, :]
    pool_bd = [_block_diag2(poolw_ref[g], poolw_ref[g + 1]) for g in range(0, len(POOL_WINDOWS), 2)]
    tril = lax.broadcasted_iota(jnp.int32, (L, L), 0) >= lax.broadcasted_iota(jnp.int32, (L, L), 1)
    w_sgu = [jnp.where(tril, sguw_ref[hd, 0:L, 0:L], 0.0).astype(_BF16) for hd in range(N_HEADS)]

    projs = []
    for c in range(MIX_CHAINS):
        base = c * R
        for q0 in range(base, base + R, NORM_ROWS):
            qs = slice(q0, q0 + NORM_ROWS)
            xn_ref[qs, :] = (_rms(x_ref[qs, :]) * g2).astype(_BF16)
        projs.append(jnp.dot(xn_ref[base:base + R, :], win_ref[...], preferred_element_type=_F32))

    for c in range(MIX_CHAINS):
        base = c * R
        rs = slice(base, base + R)
        proj = projs[c]
        p = proj[:, _OFF_P:_OFF_P + POOL_WIDTH]
        z = proj[:, _OFF_C:_OFF_C + CONV_CH] * proj[:, _OFF_H:_OFF_H + CONV_CH]
        bg = proj[:, _OFF_B:_OFF_B + CONV_CH]
        v = proj[:, _OFF_V:_OFF_V + SGU_WIDTH]

        for s in range(S):
            e = c * S + s
            if not carried:
                hist_p, hist_z = hp_ref[e], hz_ref[e]
            elif c == 0:
                hist_p, hist_z = carryp_ref[...], carryz_ref[...]
            else:
                hist_p, hist_z = extp_ref[e - 1, T:T + POOL_HIST, :], extz_ref[e - 1, T:T + CONV_HIST, :]
            extp_ref[e, 0:POOL_HIST, :] = hist_p
            extp_ref[e, POOL_HIST:POOL_HIST + T, :] = p[s * T:(s + 1) * T, :]
            extz_ref[e, 0:CONV_HIST, :] = hist_z
            extz_ref[e, CONV_HIST:CONV_HIST + T, :] = z[s * T:(s + 1) * T, :]

        t_idx = lax.broadcasted_iota(jnp.int32, (T, 1), 0) + (pos0 + (j * MIX_CHAINS + c) * T if carried else pos0)
        for s in range(S):
            e = c * S + s
            for g, w in enumerate(POOL_WINDOWS):
                cols = slice(g * HEAD_DIM, (g + 1) * HEAD_DIM)
                acc = extp_ref[e, POOL_HIST:POOL_HIST + T, cols]
                for k in range(1, w):
                    acc = acc + extp_ref[e, POOL_HIST - k:POOL_HIST - k + T, cols]
                cnt = jnp.minimum(t_idx + 1, w).astype(_F32)
                d_ref[base + s * T:base + (s + 1) * T, cols] = (
                    acc / cnt - extp_ref[e, POOL_HIST:POOL_HIST + T, cols]).astype(_BF16)
        ya = jnp.concatenate(
            [jnp.dot(d_ref[rs, 2 * i * HEAD_DIM:2 * (i + 1) * HEAD_DIM], pool_bd[i], preferred_element_type=_F32)
             for i in range(len(pool_bd))], axis=1) * pscale
        ymix_ref[rs, 0:POOL_WIDTH] = (_rms(ya) * mixg[:, 0:POOL_WIDTH]).astype(_BF16)

        c0 = POOL_WIDTH
        for s in range(S):
            e = c * S + s
            conv = (extz_ref[e, CONV_HIST - 2:CONV_HIST - 2 + T, :] * w0
                    + extz_ref[e, CONV_HIST - 1:CONV_HIST - 1 + T, :] * w1
                    + extz_ref[e, CONV_HIST:CONV_HIST + T, :] * w2)
            yb = bg[s * T:(s + 1) * T, :] * conv
            ymix_ref[base + s * T:base + (s + 1) * T, c0:c0 + CONV_CH] = (
                _rms(yb) * mixg[:, c0:c0 + CONV_CH]).astype(_BF16)

        if emit_v:
            v_ref[rs, :] = v
        vb = v.astype(_BF16)
        for hd in range(N_HEADS):
            cols = slice(hd * HEAD_DIM, (hd + 1) * HEAD_DIM)
            vcat = jnp.concatenate([vb[k * L:(k + 1) * L, cols] for k in range(units)], axis=1)
            gcat = jnp.dot(w_sgu[hd], vcat, preferred_element_type=_F32)
            bias = sgub_ref[0:L, cols]
            for k in range(units):
                rk = slice(base + k * L, base + (k + 1) * L)
                yc_ref[rk, cols] = (proj[k * L:(k + 1) * L, _OFF_U + hd * HEAD_DIM:_OFF_U + (hd + 1) * HEAD_DIM]
                                    * (gcat[:, k * HEAD_DIM:(k + 1) * HEAD_DIM] + bias))
        c0 = POOL_WIDTH + CONV_CH
        ymix_ref[rs, c0:c0 + SGU_WIDTH] = (_rms(yc_ref[rs, :]) * mixg[:, c0:c0 + SGU_WIDTH]).astype(_BF16)

        o_ref[rs, :] = jnp.dot(ymix_ref[rs, :], wout_ref[...], preferred_element_type=_F32)
        for q0 in range(base, base + R, NORM_ROWS):
            qs = slice(q0, q0 + NORM_ROWS)
            o_ref[qs, :] = x_ref[qs, :] + _rms(o_ref[qs, :]) * g3

    n_pieces = MIX_CHAINS * S
    if carried:
        ptail_ref[0] = extp_ref[n_pieces - 1, T:T + POOL_HIST, :]
        ztail_ref[0] = extz_ref[n_pieces - 1, T:T + CONV_HIST, :]
        carryp_ref[...] = extp_ref[n_pieces - 1, T:T + POOL_HIST, :]
        carryz_ref[...] = extz_ref[n_pieces - 1, T:T + CONV_HIST, :]
    else:
        for e in range(n_pieces):
            ptail_ref[e] = extp_ref[e, T:T + POOL_HIST, :]
            ztail_ref[e] = extz_ref[e, T:T + CONV_HIST, :]


def _mixer(x, hist_p, hist_z, layer, gains, mixg, win, wout, poolw, pscale, convw, sguw, sgub,
           *, n_batch, seq_len, pos0):
    carried = hist_p is None
    n_rows = x.shape[0]
    rows = MIX_CHAINS * MIX_SUB
    if carried:
        T, S = MIX_SUB, 1
        tiles_per_seq = seq_len // rows
        grid = (n_batch, tiles_per_seq)
        row_map = lambda b, j: (b * tiles_per_seq + j, 0)
        seq_map = lambda b, j: (b, 0, 0)
        seq_blk = 1
    else:
        T, S = seq_len, MIX_SUB // seq_len
        seq_blk = MIX_CHAINS * S
        grid = (n_batch // seq_blk,)
        row_map = lambda i: (i, 0)
        seq_map = lambda i: (i, 0, 0)
    nd = len(grid)
    n_pieces = MIX_CHAINS * S

    def const_spec(shape, idx):
        return pl.BlockSpec(shape, lambda *_: idx, pipeline_mode=pl.Buffered(1))

    in_specs = [pl.BlockSpec((rows, D_MODEL), row_map)]
    args = [x]
    if not carried:
        in_specs += [pl.BlockSpec((seq_blk, POOL_HIST, POOL_WIDTH), seq_map),
                     pl.BlockSpec((seq_blk, CONV_HIST, CONV_CH), seq_map)]
        args += [hist_p, hist_z]
    in_specs += [
        const_spec((None, 1, D_MODEL), (layer * N_NORMS + 2, 0, 0)),
        const_spec((None, 1, D_MODEL), (layer * N_NORMS + 3, 0, 0)),
        const_spec((None, 1, D_MODEL), (layer, 0, 0)),
        const_spec((D_MODEL, IN_WIDTH), (0, 0)),
        const_spec((D_MODEL, D_MODEL), (0, 0)),
        const_spec((None, len(POOL_WINDOWS), HEAD_DIM, HEAD_DIM), (layer, 0, 0, 0)),
        const_spec((None, 1, POOL_WIDTH), (layer, 0, 0)),
        const_spec((None, CONV_HIST, CONV_CH), (layer, 0, 0)),
        const_spec((None, N_HEADS, SGU_LEN, SGU_LEN), (layer, 0, 0, 0)),
        const_spec((None, SGU_LEN, SGU_WIDTH), (layer, 0, 0)),
    ]
    args += [gains, gains, mixg, win, wout, poolw, pscale, convw, sguw, sgub]

    out_shape = [jax.ShapeDtypeStruct((n_rows, D_MODEL), _F32),
                 jax.ShapeDtypeStruct((n_batch, POOL_HIST, POOL_WIDTH), _F32),
                 jax.ShapeDtypeStruct((n_batch, CONV_HIST, CONV_CH), _F32)]
    out_specs = [pl.BlockSpec((rows, D_MODEL), row_map),
                 pl.BlockSpec((seq_blk, POOL_HIST, POOL_WIDTH), seq_map),
                 pl.BlockSpec((seq_blk, CONV_HIST, CONV_CH), seq_map)]
    emit_v = not carried
    if emit_v:
        out_shape.append(jax.ShapeDtypeStruct((n_rows, SGU_WIDTH), _F32))
        out_specs.append(pl.BlockSpec((rows, SGU_WIDTH), row_map))

    scratch = [
        pltpu.VMEM((rows, D_MODEL), _BF16),
        pltpu.VMEM((n_pieces, POOL_HIST + T, POOL_WIDTH), _F32),
        pltpu.VMEM((n_pieces, CONV_HIST + T, CONV_CH), _F32),
        pltpu.VMEM((rows, POOL_WIDTH), _BF16),
        pltpu.VMEM((rows, SGU_WIDTH), _F32),
        pltpu.VMEM((rows, D_MODEL), _BF16),
    ]
    if carried:
        scratch += [pltpu.VMEM((POOL_HIST, POOL_WIDTH), _F32), pltpu.VMEM((CONV_HIST, CONV_CH), _F32)]

    kern = functools.partial(_mixer_kernel, n_seq=S, seq_len=T, pos0=pos0, carried=carried, emit_v=emit_v)
    return pl.pallas_call(
        kern,
        out_shape=out_shape,
        grid=grid,
        in_specs=in_specs,
        out_specs=out_specs,
        scratch_shapes=scratch,
        compiler_params=pltpu.CompilerParams(
            dimension_semantics=("arbitrary",) * nd,
            vmem_limit_bytes=V7X_VMEM_LIMIT_BYTES),
        name="mixer_prompt" if carried else "mixer_sample",
    )(*args)


def kernel(x_prompt, x_sample, state_pool, state_conv, w_in, w_out, pool_w, pool_scale, conv_w, sgu_w,
           sgu_b, ffn1_gate, ffn1_up, ffn1_down, ffn2_gate, ffn2_up, ffn2_down, norm_gains, mix_gain):
    batch, seq, _ = x_prompt.shape
    dec_batch, dec_seq, _ = x_sample.shape

    poolw_b = pool_w.astype(_BF16)
    by_chunk = lambda w: jnp.swapaxes(w.reshape(D_MODEL, D_FF // FFN_COLS, FFN_COLS), 0, 1)
    f1 = (by_chunk(ffn1_gate[0]).astype(_BF16), by_chunk(ffn1_up[0]).astype(_BF16), ffn1_down[0].astype(_BF16))
    gains = norm_gains.reshape(DEPTH * N_NORMS, 1, D_MODEL)
    mixg = mix_gain.reshape(DEPTH, 1, D_MODEL)
    pscale = pool_scale.reshape(DEPTH, 1, POOL_WIDTH)
    convw = jnp.pad(conv_w, ((0, 0), (0, CONV_HIST - conv_w.shape[1]), (0, 0)))
    sgub = jnp.repeat(jnp.swapaxes(sgu_b, 1, 2), HEAD_DIM, axis=2)
    hist_p = jnp.pad(state_pool, ((0, 0), (0, 0), (POOL_HIST - POOL_STATE, 0), (0, 0)))
    hist_z = jnp.pad(state_conv, ((0, 0), (0, 0), (CONV_HIST - CONV_STATE, 0), (0, 0)))

    yp = x_prompt.reshape(batch * seq, D_MODEL)
    ys = x_sample.reshape(dec_batch * dec_seq, D_MODEL)
    pool_p, conv_p, pool_s, conv_s, v_s = [], [], [], [], []
    for l in range(DEPTH):
        yp, win_b, wout_b, *f2 = _ffn(
            yp, gains, l, 0, *f1,
            casts=[(w_in, l, None), (w_out, l, None), (ffn2_gate, l, FFN_COLS), (ffn2_up, l, FFN_COLS),
                   (ffn2_down, l, None)])
        ys = _ffn(ys, gains, l, 0, *f1)
        mix_w = (gains, mixg, win_b, wout_b, poolw_b, pscale, convw, sgu_w, sgub)
        yp, pt, zt = _mixer(yp, None, None, l, *mix_w, n_batch=batch, seq_len=seq, pos0=0)
        ys, pts, zts, vs = _mixer(ys, hist_p[l], hist_z[l], l, *mix_w,
                                  n_batch=dec_batch, seq_len=dec_seq, pos0=PAST_LEN)
        if l + 1 < DEPTH:
            yp, *f1_next = _ffn(
                yp, gains, l, 4, *f2,
                casts=[(ffn1_gate, l + 1, FFN_COLS), (ffn1_up, l + 1, FFN_COLS), (ffn1_down, l + 1, None)])
        else:
            yp, f1_next = _ffn(yp, gains, l, 4, *f2), None
        ys = _ffn(ys, gains, l, 4, *f2)
        f1 = f1_next
        pool_p.append(pt[:, POOL_HIST - POOL_STATE:])
        conv_p.append(zt[:, CONV_HIST - CONV_STATE:])
        pool_s.append(pts[:, POOL_HIST - POOL_STATE:])
        conv_s.append(zts[:, CONV_HIST - CONV_STATE:])
        v_s.append(vs.reshape(dec_batch, dec_seq, SGU_WIDTH))
    return (yp.reshape(batch, seq, D_MODEL), ys.reshape(dec_batch, dec_seq, D_MODEL),
            jnp.stack(pool_p), jnp.stack(conv_p), jnp.stack(pool_s), jnp.stack(conv_s), jnp.stack(v_s))
```

```python
import functools

import jax
import jax.numpy as jnp
from jax import lax
from jax.experimental import pallas as pl
from jax.experimental.pallas import tpu as pltpu

D_MODEL = 2048
DEPTH = 4
HEAD_DIM = 128
POOL_WINDOWS = (2, 4, 8, 16)
POOL_WIDTH = len(POOL_WINDOWS) * HEAD_DIM
N_HEADS = 6
CONV_CH = N_HEADS * HEAD_DIM
SGU_WIDTH = N_HEADS * HEAD_DIM
SGU_LEN = 128
IN_WIDTH = POOL_WIDTH + 3 * CONV_CH + 2 * SGU_WIDTH
D_FF = 5632
POOL_STATE = 15
CONV_STATE = 2
N_NORMS = 6
PAST_LEN = 1024
EPS = 1e-6

_OFF_P = 0
_OFF_H = POOL_WIDTH
_OFF_B = _OFF_H + CONV_CH
_OFF_C = _OFF_B + CONV_CH
_OFF_U = _OFF_C + CONV_CH
_OFF_V = _OFF_U + SGU_WIDTH

POOL_HIST = 16
CONV_HIST = 8

V7X_VMEM_LIMIT_BYTES = 60 * 1024 * 1024
BF16_SUBLANES = 16
CAST_STEPS = 2

FFN_ROWS = 1024
FFN_COLS = 512
FFN_CHUNK = 256
MIX_SUB = 256
MIX_CHAINS = 2
NORM_ROWS = 64

_F32 = jnp.float32
_BF16 = jnp.bfloat16


def _rms(xf):
    return xf * lax.rsqrt(jnp.mean(xf * xf, axis=-1, keepdims=True) + EPS)


def _ffn_step(*refs, n_cast, first, last):
    x_ref, gpre_ref, gpost_ref, wg_ref, wu_ref, wd_ref = refs[:6]
    cast_src = refs[6:6 + n_cast]
    o_ref = refs[6 + n_cast]
    cast_dst = refs[7 + n_cast:7 + 2 * n_cast]
    xn_ref = refs[7 + 2 * n_cast]

    step = pl.program_id(0) * pl.num_programs(1) + pl.program_id(1)
    for src, dst in zip(cast_src, cast_dst):
        share = src.shape[0] // CAST_STEPS
        rows0 = pl.multiple_of((step % CAST_STEPS) * share, BF16_SUBLANES)
        dst[pl.ds(rows0, share), :] = src[pl.ds(rows0, share), :].astype(_BF16)

    rows = x_ref.shape[0]
    n_chunks = rows // FFN_CHUNK

    def up_phase(c):
        r0 = c * FFN_CHUNK
        if first:
            gpre = gpre_ref[...]
            for q0 in range(r0, r0 + FFN_CHUNK, NORM_ROWS):
                qs = slice(q0, q0 + NORM_ROWS)
                xn_ref[qs, :] = (_rms(x_ref[qs, :]) * gpre).astype(_BF16)
        xn = xn_ref[r0:r0 + FFN_CHUNK, :]
        gate = jnp.dot(xn, wg_ref[...], preferred_element_type=_F32)
        up = jnp.dot(xn, wu_ref[...], preferred_element_type=_F32)
        return (gate * jax.nn.sigmoid(gate) * up).astype(_BF16)

    def down_phase(c, h):
        r0 = c * FFN_CHUNK
        rs = slice(r0, r0 + FFN_CHUNK)
        part = jnp.dot(h, wd_ref[...], preferred_element_type=_F32)
        o_ref[rs, :] = part if first else o_ref[rs, :] + part
        if last:
            gpost = gpost_ref[...]
            for q0 in range(r0, r0 + FFN_CHUNK, NORM_ROWS):
                qs = slice(q0, q0 + NORM_ROWS)
                o_ref[qs, :] = x_ref[qs, :] + 0.5 * (_rms(o_ref[qs, :]) * gpost)

    h_prev = up_phase(0)
    for c in range(1, n_chunks):
        h_next = up_phase(c)
        down_phase(c - 1, h_prev)
        h_prev = h_next
    down_phase(n_chunks - 1, h_prev)


def _ffn_kernel(*refs, n_cast):
    f = pl.program_id(1)
    nf = pl.num_programs(1)
    step = functools.partial(_ffn_step, *refs, n_cast=n_cast)
    pl.when(f == 0)(functools.partial(step, first=True, last=False))
    pl.when(jnp.logical_and(f > 0, f < nf - 1))(functools.partial(step, first=False, last=False))
    pl.when(f == nf - 1)(functools.partial(step, first=False, last=True))


def _cast_slab_rows(n_rows, n_slabs):
    unit = CAST_STEPS * BF16_SUBLANES
    r = unit
    while n_rows % r or n_rows // r > n_slabs:
        r += unit
    return r


def _ffn(x, gains, layer, k_pre, wg, wu, wd, casts=()):
    n_rows = x.shape[0]
    tm = min(FFN_ROWS, n_rows)
    n_f = D_FF // FFN_COLS
    grid = (n_rows // tm, n_f)
    n_steps = grid[0] * grid[1]
    gain_spec = lambda k: pl.BlockSpec((None, 1, D_MODEL), lambda i, f, k=k: (layer * N_NORMS + k, 0, 0))
    in_specs = [
        pl.BlockSpec((tm, D_MODEL), lambda i, f: (i, 0)),
        gain_spec(k_pre),
        gain_spec(k_pre + 1),
        pl.BlockSpec((D_MODEL, FFN_COLS), lambda i, f: (0, f)),
        pl.BlockSpec((D_MODEL, FFN_COLS), lambda i, f: (0, f)),
        pl.BlockSpec((FFN_COLS, D_MODEL), lambda i, f: (f, 0)),
    ]
    out_shape = [jax.ShapeDtypeStruct((n_rows, D_MODEL), _F32)]
    out_specs = [pl.BlockSpec((tm, D_MODEL), lambda i, f: (i, 0))]
    for w, wl in casts:
        _, r_total, cols = w.shape
        r = _cast_slab_rows(r_total, n_steps // CAST_STEPS)
        last_blk = r_total // r - 1
        assert (last_blk + 1) * CAST_STEPS <= n_steps
        slab = lambda i, f, last_blk=last_blk: jnp.minimum((i * n_f + f) // CAST_STEPS, last_blk)
        in_specs.append(pl.BlockSpec((None, r, cols), lambda i, f, wl=wl, slab=slab: (wl, slab(i, f), 0)))
        out_shape.append(jax.ShapeDtypeStruct((r_total, cols), _BF16))
        out_specs.append(pl.BlockSpec((r, cols), lambda i, f, slab=slab: (slab(i, f), 0)))
    res = pl.pallas_call(
        functools.partial(_ffn_kernel, n_cast=len(casts)),
        out_shape=out_shape,
        grid=grid,
        in_specs=in_specs,
        out_specs=out_specs,
        scratch_shapes=[pltpu.VMEM((tm, D_MODEL), _BF16)],
        compiler_params=pltpu.CompilerParams(
            dimension_semantics=("arbitrary", "arbitrary"),
            vmem_limit_bytes=V7X_VMEM_LIMIT_BYTES),
        name="ffn",
    )(x, gains, gains, wg, wu, wd, *[c[0] for c in casts])
    return res[0] if not casts else res


def _block_diag2(a, b):
    z = jnp.zeros_like(a)
    return jnp.concatenate([jnp.concatenate([a, z], axis=1), jnp.concatenate([z, b], axis=1)], axis=0)


def _mixer_kernel(*refs, n_seq, seq_len, pos0, carried, emit_v):
    it = iter(refs)
    x_ref = next(it)
    if not carried:
        hp_ref = next(it)
        hz_ref = next(it)
    g2_ref, g3_ref, mixg_ref = next(it), next(it), next(it)
    win_ref, wout_ref, poolw_ref, pscale_ref = next(it), next(it), next(it), next(it)
    convw_ref, sguw_ref, sgub_ref = next(it), next(it), next(it)
    o_ref, ptail_ref, ztail_ref = next(it), next(it), next(it)
    if emit_v:
        v_ref = next(it)
    xn_ref, extp_ref, extz_ref, d_ref, yc_ref, ymix_ref = (next(it) for _ in range(6))
    if carried:
        carryp_ref, carryz_ref = next(it), next(it)

    S, T = n_seq, seq_len
    R = S * T
    L = min(T, SGU_LEN)
    units = R // L
    j = pl.program_id(1) if carried else 0

    if carried:
        @pl.when(j == 0)
        def _():
            carryp_ref[...] = jnp.zeros_like(carryp_ref)
            carryz_ref[...] = jnp.zeros_like(carryz_ref)

    g2, g3, mixg, pscale = g2_ref[...], g3_ref[...], mixg_ref[...], pscale_ref[...]
    w0, w1, w2 = convw_ref[0:1, :], convw_ref[1:2, :], convw_ref[2:3, :]
    pool_bd = [_block_diag2(poolw_ref[g], poolw_ref[g + 1]) for g in range(0, len(POOL_WINDOWS), 2)]
    tril = lax.broadcasted_iota(jnp.int32, (L, L), 0) >= lax.broadcasted_iota(jnp.int32, (L, L), 1)
    w_sgu = [jnp.where(tril, sguw_ref[hd, 0:L, 0:L], 0.0).astype(_BF16) for hd in range(N_HEADS)]

    projs = []
    for c in range(MIX_CHAINS):
        base = c * R
        for q0 in range(base, base + R, NORM_ROWS):
            qs = slice(q0, q0 + NORM_ROWS)
            xn_ref[qs, :] = (_rms(x_ref[qs, :]) * g2).astype(_BF16)
        projs.append(jnp.dot(xn_ref[base:base + R, :], win_ref[...], preferred_element_type=_F32))

    for c in range(MIX_CHAINS):
        base = c * R
        rs = slice(base, base + R)
        proj = projs[c]
        p = proj[:, _OFF_P:_OFF_P + POOL_WIDTH]
        z = proj[:, _OFF_C:_OFF_C + CONV_CH] * proj[:, _OFF_H:_OFF_H + CONV_CH]
        bg = proj[:, _OFF_B:_OFF_B + CONV_CH]
        v = proj[:, _OFF_V:_OFF_V + SGU_WIDTH]

        for s in range(S):
            e = c * S + s
            if not carried:
                hist_p, hist_z = hp_ref[e], hz_ref[e]
            elif c == 0:
                hist_p, hist_z = carryp_ref[...], carryz_ref[...]
            else:
                hist_p, hist_z = extp_ref[e - 1, T:T + POOL_HIST, :], extz_ref[e - 1, T:T + CONV_HIST, :]
            extp_ref[e, 0:POOL_HIST, :] = hist_p
            extp_ref[e, POOL_HIST:POOL_HIST + T, :] = p[s * T:(s + 1) * T, :]
            extz_ref[e, 0:CONV_HIST, :] = hist_z
            extz_ref[e, CONV_HIST:CONV_HIST + T, :] = z[s * T:(s + 1) * T, :]

        t_idx = lax.broadcasted_iota(jnp.int32, (T, 1), 0) + (pos0 + (j * MIX_CHAINS + c) * T if carried else pos0)
        for s in range(S):
            e = c * S + s
            for g, w in enumerate(POOL_WINDOWS):
                cols = slice(g * HEAD_DIM, (g + 1) * HEAD_DIM)
                acc = extp_ref[e, POOL_HIST:POOL_HIST + T, cols]
                for k in range(1, w):
                    acc = acc + extp_ref[e, POOL_HIST - k:POOL_HIST - k + T, cols]
                cnt = jnp.minimum(t_idx + 1, w).astype(_F32)
                d_ref[base + s * T:base + (s + 1) * T, cols] = (
                    acc / cnt - extp_ref[e, POOL_HIST:POOL_HIST + T, cols]).astype(_BF16)
        ya = jnp.concatenate(
            [jnp.dot(d_ref[rs, 2 * i * HEAD_DIM:2 * (i + 1) * HEAD_DIM], pool_bd[i], preferred_element_type=_F32)
             for i in range(len(pool_bd))], axis=1) * pscale
        ymix_ref[rs, 0:POOL_WIDTH] = (_rms(ya) * mixg[:, 0:POOL_WIDTH]).astype(_BF16)

        c0 = POOL_WIDTH
        for s in range(S):
            e = c * S + s
            conv = (extz_ref[e, CONV_HIST - 2:CONV_HIST - 2 + T, :] * w0
                    + extz_ref[e, CONV_HIST - 1:CONV_HIST - 1 + T, :] * w1
                    + extz_ref[e, CONV_HIST:CONV_HIST + T, :] * w2)
            yb = bg[s * T:(s + 1) * T, :] * conv
            ymix_ref[base + s * T:base + (s + 1) * T, c0:c0 + CONV_CH] = (
                _rms(yb) * mixg[:, c0:c0 + CONV_CH]).astype(_BF16)

        if emit_v:
            v_ref[rs, :] = v
        vb = v.astype(_BF16)
        for hd in range(N_HEADS):
            cols = slice(hd * HEAD_DIM, (hd + 1) * HEAD_DIM)
            vcat = jnp.concatenate([vb[k * L:(k + 1) * L, cols] for k in range(units)], axis=1)
            gcat = jnp.dot(w_sgu[hd], vcat, preferred_element_type=_F32)
            bias = sgub_ref[0:L, cols]
            for k in range(units):
                rk = slice(base + k * L, base + (k + 1) * L)
                yc_ref[rk, cols] = (proj[k * L:(k + 1) * L, _OFF_U + hd * HEAD_DIM:_OFF_U + (hd + 1) * HEAD_DIM]
                                    * (gcat[:, k * HEAD_DIM:(k + 1) * HEAD_DIM] + bias))
        c0 = POOL_WIDTH + CONV_CH
        ymix_ref[rs, c0:c0 + SGU_WIDTH] = (_rms(yc_ref[rs, :]) * mixg[:, c0:c0 + SGU_WIDTH]).astype(_BF16)

        o_ref[rs, :] = jnp.dot(ymix_ref[rs, :], wout_ref[...], preferred_element_type=_F32)
        for q0 in range(base, base + R, NORM_ROWS):
            qs = slice(q0, q0 + NORM_ROWS)
            o_ref[qs, :] = x_ref[qs, :] + _rms(o_ref[qs, :]) * g3

    n_pieces = MIX_CHAINS * S
    if carried:
        ptail_ref[0] = extp_ref[n_pieces - 1, T:T + POOL_HIST, :]
        ztail_ref[0] = extz_ref[n_pieces - 1, T:T + CONV_HIST, :]
        carryp_ref[...] = extp_ref[n_pieces - 1, T:T + POOL_HIST, :]
        carryz_ref[...] = extz_ref[n_pieces - 1, T:T + CONV_HIST, :]
    else:
        for e in range(n_pieces):
            ptail_ref[e] = extp_ref[e, T:T + POOL_HIST, :]
            ztail_ref[e] = extz_ref[e, T:T + CONV_HIST, :]


def _mixer(x, hist_p, hist_z, layer, gains, mixg, win, wout, poolw, pscale, convw, sguw, sgub,
           *, n_batch, seq_len, pos0):
    carried = hist_p is None
    n_rows = x.shape[0]
    rows = MIX_CHAINS * MIX_SUB
    if carried:
        T, S = MIX_SUB, 1
        tiles_per_seq = seq_len // rows
        grid = (n_batch, tiles_per_seq)
        row_map = lambda b, j: (b * tiles_per_seq + j, 0)
        seq_map = lambda b, j: (b, 0, 0)
        seq_blk = 1
    else:
        T, S = seq_len, MIX_SUB // seq_len
        seq_blk = MIX_CHAINS * S
        grid = (n_batch // seq_blk,)
        row_map = lambda i: (i, 0)
        seq_map = lambda i: (i, 0, 0)
    nd = len(grid)
    n_pieces = MIX_CHAINS * S

    def const_spec(shape, idx):
        return pl.BlockSpec(shape, lambda *_: idx, pipeline_mode=pl.Buffered(1))

    in_specs = [pl.BlockSpec((rows, D_MODEL), row_map)]
    args = [x]
    if not carried:
        in_specs += [pl.BlockSpec((seq_blk, POOL_HIST, POOL_WIDTH), seq_map),
                     pl.BlockSpec((seq_blk, CONV_HIST, CONV_CH), seq_map)]
        args += [hist_p, hist_z]
    in_specs += [
        const_spec((None, 1, D_MODEL), (layer * N_NORMS + 2, 0, 0)),
        const_spec((None, 1, D_MODEL), (layer * N_NORMS + 3, 0, 0)),
        const_spec((None, 1, D_MODEL), (layer, 0, 0)),
        const_spec((D_MODEL, IN_WIDTH), (0, 0)),
        const_spec((D_MODEL, D_MODEL), (0, 0)),
        const_spec((None, len(POOL_WINDOWS), HEAD_DIM, HEAD_DIM), (layer, 0, 0, 0)),
        const_spec((None, 1, POOL_WIDTH), (layer, 0, 0)),
        const_spec((None, CONV_HIST, CONV_CH), (layer, 0, 0)),
        const_spec((None, N_HEADS, SGU_LEN, SGU_LEN), (layer, 0, 0, 0)),
        const_spec((None, SGU_LEN, SGU_WIDTH), (layer, 0, 0)),
    ]
    args += [gains, gains, mixg, win, wout, poolw, pscale, convw, sguw, sgub]

    out_shape = [jax.ShapeDtypeStruct((n_rows, D_MODEL), _F32),
                 jax.ShapeDtypeStruct((n_batch, POOL_HIST, POOL_WIDTH), _F32),
                 jax.ShapeDtypeStruct((n_batch, CONV_HIST, CONV_CH), _F32)]
    out_specs = [pl.BlockSpec((rows, D_MODEL), row_map),
                 pl.BlockSpec((seq_blk, POOL_HIST, POOL_WIDTH), seq_map),
                 pl.BlockSpec((seq_blk, CONV_HIST, CONV_CH), seq_map)]
    emit_v = not carried
    if emit_v:
        out_shape.append(jax.ShapeDtypeStruct((n_rows, SGU_WIDTH), _F32))
        out_specs.append(pl.BlockSpec((rows, SGU_WIDTH), row_map))

    scratch = [
        pltpu.VMEM((rows, D_MODEL), _BF16),
        pltpu.VMEM((n_pieces, POOL_HIST + T, POOL_WIDTH), _F32),
        pltpu.VMEM((n_pieces, CONV_HIST + T, CONV_CH), _F32),
        pltpu.VMEM((rows, POOL_WIDTH), _BF16),
        pltpu.VMEM((rows, SGU_WIDTH), _F32),
        pltpu.VMEM((rows, D_MODEL), _BF16),
    ]
    if carried:
        scratch += [pltpu.VMEM((POOL_HIST, POOL_WIDTH), _F32), pltpu.VMEM((CONV_HIST, CONV_CH), _F32)]

    kern = functools.partial(_mixer_kernel, n_seq=S, seq_len=T, pos0=pos0, carried=carried, emit_v=emit_v)
    return pl.pallas_call(
        kern,
        out_shape=out_shape,
        grid=grid,
        in_specs=in_specs,
        out_specs=out_specs,
        scratch_shapes=scratch,
        compiler_params=pltpu.CompilerParams(
            dimension_semantics=("arbitrary",) * nd,
            vmem_limit_bytes=V7X_VMEM_LIMIT_BYTES),
        name="mixer_prompt" if carried else "mixer_sample",
    )(*args)


def kernel(x_prompt, x_sample, state_pool, state_conv, w_in, w_out, pool_w, pool_scale, conv_w, sgu_w,
           sgu_b, ffn1_gate, ffn1_up, ffn1_down, ffn2_gate, ffn2_up, ffn2_down, norm_gains, mix_gain):
    batch, seq, _ = x_prompt.shape
    dec_batch, dec_seq, _ = x_sample.shape

    poolw_b = pool_w.astype(_BF16)
    f1 = tuple(w[0].astype(_BF16) for w in (ffn1_gate, ffn1_up, ffn1_down))
    gains = norm_gains.reshape(DEPTH * N_NORMS, 1, D_MODEL)
    mixg = mix_gain.reshape(DEPTH, 1, D_MODEL)
    pscale = pool_scale.reshape(DEPTH, 1, POOL_WIDTH)
    convw = jnp.pad(conv_w, ((0, 0), (0, CONV_HIST - conv_w.shape[1]), (0, 0)))
    sgub = jnp.repeat(jnp.swapaxes(sgu_b, 1, 2), HEAD_DIM, axis=2)
    hist_p = jnp.pad(state_pool, ((0, 0), (0, 0), (POOL_HIST - POOL_STATE, 0), (0, 0)))
    hist_z = jnp.pad(state_conv, ((0, 0), (0, 0), (CONV_HIST - CONV_STATE, 0), (0, 0)))

    yp = x_prompt.reshape(batch * seq, D_MODEL)
    ys = x_sample.reshape(dec_batch * dec_seq, D_MODEL)
    pool_p, conv_p, pool_s, conv_s, v_s = [], [], [], [], []
    for l in range(DEPTH):
        yp, win_b, wout_b, *f2 = _ffn(
            yp, gains, l, 0, *f1,
            casts=[(w_in, l), (w_out, l), (ffn2_gate, l), (ffn2_up, l), (ffn2_down, l)])
        ys = _ffn(ys, gains, l, 0, *f1)
        mix_w = (gains, mixg, win_b, wout_b, poolw_b, pscale, convw, sgu_w, sgub)
        yp, pt, zt = _mixer(yp, None, None, l, *mix_w, n_batch=batch, seq_len=seq, pos0=0)
        ys, pts, zts, vs = _mixer(ys, hist_p[l], hist_z[l], l, *mix_w,
                                  n_batch=dec_batch, seq_len=dec_seq, pos0=PAST_LEN)
        if l + 1 < DEPTH:
            yp, *f1_next = _ffn(
                yp, gains, l, 4, *f2,
                casts=[(ffn1_gate, l + 1), (ffn1_up, l + 1), (ffn1_down, l + 1)])
        else:
            yp, f1_next = _ffn(yp, gains, l, 4, *f2), None
        ys = _ffn(ys, gains, l, 4, *f2)
        f1 = f1_next
        pool_p.append(pt[:, POOL_HIST - POOL_STATE:])
        conv_p.append(zt[:, CONV_HIST - CONV_STATE:])
        pool_s.append(pts[:, POOL_HIST - POOL_STATE:])
        conv_s.append(zts[:, CONV_HIST - CONV_STATE:])
        v_s.append(vs.reshape(dec_batch, dec_seq, SGU_WIDTH))
    return (yp.reshape(batch, seq, D_MODEL), ys.reshape(dec_batch, dec_seq, D_MODEL),
            jnp.stack(pool_p), jnp.stack(conv_p), jnp.stack(pool_s), jnp.stack(conv_s), jnp.stack(v_s))
```

```python
import functools

import jax
import jax.numpy as jnp
from jax import lax
from jax.experimental import pallas as pl
from jax.experimental.pallas import tpu as pltpu

D_MODEL = 2048
DEPTH = 4
HEAD_DIM = 128
POOL_WINDOWS = (2, 4, 8, 16)
POOL_WIDTH = len(POOL_WINDOWS) * HEAD_DIM
N_HEADS = 6
CONV_CH = N_HEADS * HEAD_DIM
SGU_WIDTH = N_HEADS * HEAD_DIM
SGU_LEN = 128
IN_WIDTH = POOL_WIDTH + 3 * CONV_CH + 2 * SGU_WIDTH
D_FF = 5632
POOL_STATE = 15
CONV_STATE = 2
N_NORMS = 6
PAST_LEN = 1024
EPS = 1e-6

_OFF_P = 0
_OFF_H = POOL_WIDTH
_OFF_B = _OFF_H + CONV_CH
_OFF_C = _OFF_B + CONV_CH
_OFF_U = _OFF_C + CONV_CH
_OFF_V = _OFF_U + SGU_WIDTH

POOL_HIST = 16
CONV_HIST = 8

V7X_VMEM_LIMIT_BYTES = 60 * 1024 * 1024
BF16_SUBLANES = 16

FFN_ROWS = 1024
FFN_COLS = 512
FFN_CHUNK = 256
MIX_SUB = 256
MIX_CHAINS = 2
NORM_ROWS = 64

_F32 = jnp.float32
_BF16 = jnp.bfloat16


def _rms(xf):
    return xf * lax.rsqrt(jnp.mean(xf * xf, axis=-1, keepdims=True) + EPS)


def _ffn_step(*refs, n_cast, first, last):
    x_ref, gpre_ref, gpost_ref, wg_ref, wu_ref, wd_ref = refs[:6]
    cast_src = refs[6:6 + n_cast]
    o_ref = refs[6 + n_cast]
    cast_dst = refs[7 + n_cast:7 + 2 * n_cast]
    xn_ref = refs[7 + 2 * n_cast]

    for src, dst in zip(cast_src, cast_dst):
        dst[...] = src[0].astype(_BF16)

    rows = x_ref.shape[0]
    n_chunks = rows // FFN_CHUNK

    def up_phase(c):
        r0 = c * FFN_CHUNK
        if first:
            gpre = gpre_ref[0]
            for q0 in range(r0, r0 + FFN_CHUNK, NORM_ROWS):
                qs = slice(q0, q0 + NORM_ROWS)
                xn_ref[qs, :] = (_rms(x_ref[qs, :]) * gpre).astype(_BF16)
        xn = xn_ref[r0:r0 + FFN_CHUNK, :]
        gate = jnp.dot(xn, wg_ref[...], preferred_element_type=_F32)
        up = jnp.dot(xn, wu_ref[...], preferred_element_type=_F32)
        return (gate * jax.nn.sigmoid(gate) * up).astype(_BF16)

    def down_phase(c, h):
        r0 = c * FFN_CHUNK
        rs = slice(r0, r0 + FFN_CHUNK)
        part = jnp.dot(h, wd_ref[...], preferred_element_type=_F32)
        o_ref[rs, :] = part if first else o_ref[rs, :] + part
        if last:
            gpost = gpost_ref[0]
            for q0 in range(r0, r0 + FFN_CHUNK, NORM_ROWS):
                qs = slice(q0, q0 + NORM_ROWS)
                o_ref[qs, :] = x_ref[qs, :] + 0.5 * (_rms(o_ref[qs, :]) * gpost)

    h_prev = up_phase(0)
    for c in range(1, n_chunks):
        h_next = up_phase(c)
        down_phase(c - 1, h_prev)
        h_prev = h_next
    down_phase(n_chunks - 1, h_prev)


def _ffn_kernel(*refs, n_cast):
    f = pl.program_id(1)
    nf = pl.num_programs(1)
    step = functools.partial(_ffn_step, *refs, n_cast=n_cast)
    pl.when(f == 0)(functools.partial(step, first=True, last=False))
    pl.when(jnp.logical_and(f > 0, f < nf - 1))(functools.partial(step, first=False, last=False))
    pl.when(f == nf - 1)(functools.partial(step, first=False, last=True))


def _cast_rows_per_step(n_rows, n_steps):
    r = BF16_SUBLANES
    while n_rows % r or n_rows // r > n_steps:
        r += BF16_SUBLANES
    return r


def _ffn(x, gains, layer, k_pre, wg, wu, wd, casts=()):
    n_rows = x.shape[0]
    tm = min(FFN_ROWS, n_rows)
    n_f = D_FF // FFN_COLS
    grid = (n_rows // tm, n_f)
    n_steps = grid[0] * grid[1]
    gain_spec = lambda k: pl.BlockSpec((1, 1, D_MODEL), lambda i, f, k=k: (layer * N_NORMS + k, 0, 0))
    in_specs = [
        pl.BlockSpec((tm, D_MODEL), lambda i, f: (i, 0)),
        gain_spec(k_pre),
        gain_spec(k_pre + 1),
        pl.BlockSpec((D_MODEL, FFN_COLS), lambda i, f: (0, f)),
        pl.BlockSpec((D_MODEL, FFN_COLS), lambda i, f: (0, f)),
        pl.BlockSpec((FFN_COLS, D_MODEL), lambda i, f: (f, 0)),
    ]
    out_shape = [jax.ShapeDtypeStruct((n_rows, D_MODEL), _F32)]
    out_specs = [pl.BlockSpec((tm, D_MODEL), lambda i, f: (i, 0))]
    for w, wl in casts:
        _, r_total, cols = w.shape
        r = _cast_rows_per_step(r_total, n_steps)
        last_blk = r_total // r - 1
        in_specs.append(pl.BlockSpec(
            (1, r, cols), lambda i, f, wl=wl, last_blk=last_blk: (wl, jnp.minimum(i * n_f + f, last_blk), 0)))
        out_shape.append(jax.ShapeDtypeStruct((r_total, cols), _BF16))
        out_specs.append(pl.BlockSpec(
            (r, cols), lambda i, f, last_blk=last_blk: (jnp.minimum(i * n_f + f, last_blk), 0)))
    n_in, n_cast = len(in_specs), len(casts)

    def outer(*refs):
        xn_ref = refs[-1]
        body = lambda *blocks: _ffn_kernel(*blocks, xn_ref, n_cast=n_cast)
        pltpu.emit_pipeline(body, grid=grid, in_specs=in_specs, out_specs=out_specs)(*refs[:-1])

    any_spec = pl.BlockSpec(memory_space=pl.ANY)
    res = pl.pallas_call(
        outer,
        out_shape=out_shape,
        in_specs=[any_spec] * n_in,
        out_specs=[any_spec] * len(out_shape),
        scratch_shapes=[pltpu.VMEM((tm, D_MODEL), _BF16)],
        compiler_params=pltpu.CompilerParams(vmem_limit_bytes=V7X_VMEM_LIMIT_BYTES),
        name="ffn",
    )(x, gains, gains, wg, wu, wd, *[w for w, _ in casts])
    return res[0] if not casts else res


def _block_diag2(a, b):
    z = jnp.zeros_like(a)
    return jnp.concatenate([jnp.concatenate([a, z], axis=1), jnp.concatenate([z, b], axis=1)], axis=0)


def _mixer_kernel(*refs, n_seq, seq_len, pos0, carried, emit_v):
    it = iter(refs)
    x_ref = next(it)
    if not carried:
        hp_ref = next(it)
        hz_ref = next(it)
    g2_ref, g3_ref, mixg_ref = next(it), next(it), next(it)
    win_ref, wout_ref, poolw_ref, pscale_ref = next(it), next(it), next(it), next(it)
    convw_ref, sguw_ref, sgub_ref = next(it), next(it), next(it)
    o_ref, ptail_ref, ztail_ref = next(it), next(it), next(it)
    if emit_v:
        v_ref = next(it)
    xn_ref, extp_ref, extz_ref, d_ref, yc_ref, ymix_ref = (next(it) for _ in range(6))
    if carried:
        carryp_ref, carryz_ref = next(it), next(it)

    S, T = n_seq, seq_len
    R = S * T
    L = min(T, SGU_LEN)
    units = R // L
    j = pl.program_id(1) if carried else 0

    if carried:
        @pl.when(j == 0)
        def _():
            carryp_ref[...] = jnp.zeros_like(carryp_ref)
            carryz_ref[...] = jnp.zeros_like(carryz_ref)

    g2, g3, mixg, pscale = g2_ref[...], g3_ref[...], mixg_ref[...], pscale_ref[...]
    w0, w1, w2 = convw_ref[0:1, :], convw_ref[1:2, :], convw_ref[2:3, :]
    pool_bd = [_block_diag2(poolw_ref[g], poolw_ref[g + 1]) for g in range(0, len(POOL_WINDOWS), 2)]
    tril = lax.broadcasted_iota(jnp.int32, (L, L), 0) >= lax.broadcasted_iota(jnp.int32, (L, L), 1)
    w_sgu = [jnp.where(tril, sguw_ref[hd, 0:L, 0:L], 0.0).astype(_BF16) for hd in range(N_HEADS)]

    for c in range(MIX_CHAINS):
        base = c * R
        rs = slice(base, base + R)
        for q0 in range(base, base + R, NORM_ROWS):
            qs = slice(q0, q0 + NORM_ROWS)
            xn_ref[qs, :] = (_rms(x_ref[qs, :]) * g2).astype(_BF16)
        proj = jnp.dot(xn_ref[rs, :], win_ref[...], preferred_element_type=_F32)
        p = proj[:, _OFF_P:_OFF_P + POOL_WIDTH]
        z = proj[:, _OFF_C:_OFF_C + CONV_CH] * proj[:, _OFF_H:_OFF_H + CONV_CH]
        bg = proj[:, _OFF_B:_OFF_B + CONV_CH]
        v = proj[:, _OFF_V:_OFF_V + SGU_WIDTH]

        for s in range(S):
            e = c * S + s
            if not carried:
                hist_p, hist_z = hp_ref[e], hz_ref[e]
            elif c == 0:
                hist_p, hist_z = carryp_ref[...], carryz_ref[...]
            else:
                hist_p, hist_z = extp_ref[e - 1, T:T + POOL_HIST, :], extz_ref[e - 1, T:T + CONV_HIST, :]
            extp_ref[e, 0:POOL_HIST, :] = hist_p
            extp_ref[e, POOL_HIST:POOL_HIST + T, :] = p[s * T:(s + 1) * T, :]
            extz_ref[e, 0:CONV_HIST, :] = hist_z
            extz_ref[e, CONV_HIST:CONV_HIST + T, :] = z[s * T:(s + 1) * T, :]

        t_idx = lax.broadcasted_iota(jnp.int32, (T, 1), 0) + (pos0 + (j * MIX_CHAINS + c) * T if carried else pos0)
        for s in range(S):
            e = c * S + s
            for g, w in enumerate(POOL_WINDOWS):
                cols = slice(g * HEAD_DIM, (g + 1) * HEAD_DIM)
                acc = extp_ref[e, POOL_HIST:POOL_HIST + T, cols]
                for k in range(1, w):
                    acc = acc + extp_ref[e, POOL_HIST - k:POOL_HIST - k + T, cols]
                cnt = jnp.minimum(t_idx + 1, w).astype(_F32)
                d_ref[base + s * T:base + (s + 1) * T, cols] = (
                    acc / cnt - extp_ref[e, POOL_HIST:POOL_HIST + T, cols]).astype(_BF16)
        ya = jnp.concatenate(
            [jnp.dot(d_ref[rs, 2 * i * HEAD_DIM:2 * (i + 1) * HEAD_DIM], pool_bd[i], preferred_element_type=_F32)
             for i in range(len(pool_bd))], axis=1) * pscale
        ymix_ref[rs, 0:POOL_WIDTH] = (_rms(ya) * mixg[:, 0:POOL_WIDTH]).astype(_BF16)

        c0 = POOL_WIDTH
        for s in range(S):
            e = c * S + s
            conv = (extz_ref[e, CONV_HIST - 2:CONV_HIST - 2 + T, :] * w0
                    + extz_ref[e, CONV_HIST - 1:CONV_HIST - 1 + T, :] * w1
                    + extz_ref[e, CONV_HIST:CONV_HIST + T, :] * w2)
            yb = bg[s * T:(s + 1) * T, :] * conv
            ymix_ref[base + s * T:base + (s + 1) * T, c0:c0 + CONV_CH] = (
                _rms(yb) * mixg[:, c0:c0 + CONV_CH]).astype(_BF16)

        if emit_v:
            v_ref[rs, :] = v
        vb = v.astype(_BF16)
        for hd in range(N_HEADS):
            cols = slice(hd * HEAD_DIM, (hd + 1) * HEAD_DIM)
            vcat = jnp.concatenate([vb[k * L:(k + 1) * L, cols] for k in range(units)], axis=1)
            gcat = jnp.dot(w_sgu[hd], vcat, preferred_element_type=_F32)
            bias = sgub_ref[0:L, cols]
            for k in range(units):
                rk = slice(base + k * L, base + (k + 1) * L)
                yc_ref[rk, cols] = (proj[k * L:(k + 1) * L, _OFF_U + hd * HEAD_DIM:_OFF_U + (hd + 1) * HEAD_DIM]
                                    * (gcat[:, k * HEAD_DIM:(k + 1) * HEAD_DIM] + bias))
        c0 = POOL_WIDTH + CONV_CH
        ymix_ref[rs, c0:c0 + SGU_WIDTH] = (_rms(yc_ref[rs, :]) * mixg[:, c0:c0 + SGU_WIDTH]).astype(_BF16)

        o_ref[rs, :] = jnp.dot(ymix_ref[rs, :], wout_ref[...], preferred_element_type=_F32)
        for q0 in range(base, base + R, NORM_ROWS):
            qs = slice(q0, q0 + NORM_ROWS)
            o_ref[qs, :] = x_ref[qs, :] + _rms(o_ref[qs, :]) * g3

    n_pieces = MIX_CHAINS * S
    if carried:
        ptail_ref[0] = extp_ref[n_pieces - 1, T:T + POOL_HIST, :]
        ztail_ref[0] = extz_ref[n_pieces - 1, T:T + CONV_HIST, :]
        carryp_ref[...] = extp_ref[n_pieces - 1, T:T + POOL_HIST, :]
        carryz_ref[...] = extz_ref[n_pieces - 1, T:T + CONV_HIST, :]
    else:
        for e in range(n_pieces):
            ptail_ref[e] = extp_ref[e, T:T + POOL_HIST, :]
            ztail_ref[e] = extz_ref[e, T:T + CONV_HIST, :]


def _mixer(x, hist_p, hist_z, layer, gains, mixg, win, wout, poolw, pscale, convw, sguw, sgub,
           *, n_batch, seq_len, pos0):
    carried = hist_p is None
    n_rows = x.shape[0]
    rows = MIX_CHAINS * MIX_SUB
    if carried:
        T, S = MIX_SUB, 1
        tiles_per_seq = seq_len // rows
        grid = (n_batch, tiles_per_seq)
        row_map = lambda b, j: (b * tiles_per_seq + j, 0)
        seq_map = lambda b, j: (b, 0, 0)
        seq_blk = 1
    else:
        T, S = seq_len, MIX_SUB // seq_len
        seq_blk = MIX_CHAINS * S
        grid = (n_batch // seq_blk,)
        row_map = lambda i: (i, 0)
        seq_map = lambda i: (i, 0, 0)
    nd = len(grid)
    n_pieces = MIX_CHAINS * S

    def const_spec(shape, idx):
        return pl.BlockSpec(shape, lambda *_: idx, pipeline_mode=pl.Buffered(1))

    in_specs = [pl.BlockSpec((rows, D_MODEL), row_map)]
    args = [x]
    if not carried:
        in_specs += [pl.BlockSpec((seq_blk, POOL_HIST, POOL_WIDTH), seq_map),
                     pl.BlockSpec((seq_blk, CONV_HIST, CONV_CH), seq_map)]
        args += [hist_p, hist_z]
    in_specs += [
        const_spec((None, 1, D_MODEL), (layer * N_NORMS + 2, 0, 0)),
        const_spec((None, 1, D_MODEL), (layer * N_NORMS + 3, 0, 0)),
        const_spec((None, 1, D_MODEL), (layer, 0, 0)),
        const_spec((D_MODEL, IN_WIDTH), (0, 0)),
        const_spec((D_MODEL, D_MODEL), (0, 0)),
        const_spec((None, len(POOL_WINDOWS), HEAD_DIM, HEAD_DIM), (layer, 0, 0, 0)),
        const_spec((None, 1, POOL_WIDTH), (layer, 0, 0)),
        const_spec((None, CONV_HIST, CONV_CH), (layer, 0, 0)),
        const_spec((None, N_HEADS, SGU_LEN, SGU_LEN), (layer, 0, 0, 0)),
        const_spec((None, SGU_LEN, SGU_WIDTH), (layer, 0, 0)),
    ]
    args += [gains, gains, mixg, win, wout, poolw, pscale, convw, sguw, sgub]

    out_shape = [jax.ShapeDtypeStruct((n_rows, D_MODEL), _F32),
                 jax.ShapeDtypeStruct((n_batch, POOL_HIST, POOL_WIDTH), _F32),
                 jax.ShapeDtypeStruct((n_batch, CONV_HIST, CONV_CH), _F32)]
    out_specs = [pl.BlockSpec((rows, D_MODEL), row_map),
                 pl.BlockSpec((seq_blk, POOL_HIST, POOL_WIDTH), seq_map),
                 pl.BlockSpec((seq_blk, CONV_HIST, CONV_CH), seq_map)]
    emit_v = not carried
    if emit_v:
        out_shape.append(jax.ShapeDtypeStruct((n_rows, SGU_WIDTH), _F32))
        out_specs.append(pl.BlockSpec((rows, SGU_WIDTH), row_map))

    scratch = [
        pltpu.VMEM((rows, D_MODEL), _BF16),
        pltpu.VMEM((n_pieces, POOL_HIST + T, POOL_WIDTH), _F32),
        pltpu.VMEM((n_pieces, CONV_HIST + T, CONV_CH), _F32),
        pltpu.VMEM((rows, POOL_WIDTH), _BF16),
        pltpu.VMEM((rows, SGU_WIDTH), _F32),
        pltpu.VMEM((rows, D_MODEL), _BF16),
    ]
    if carried:
        scratch += [pltpu.VMEM((POOL_HIST, POOL_WIDTH), _F32), pltpu.VMEM((CONV_HIST, CONV_CH), _F32)]

    kern = functools.partial(_mixer_kernel, n_seq=S, seq_len=T, pos0=pos0, carried=carried, emit_v=emit_v)
    return pl.pallas_call(
        kern,
        out_shape=out_shape,
        grid=grid,
        in_specs=in_specs,
        out_specs=out_specs,
        scratch_shapes=scratch,
        compiler_params=pltpu.CompilerParams(
            dimension_semantics=("arbitrary",) * nd,
            vmem_limit_bytes=V7X_VMEM_LIMIT_BYTES),
        name="mixer_prompt" if carried else "mixer_sample",
    )(*args)


def kernel(x_prompt, x_sample, state_pool, state_conv, w_in, w_out, pool_w, pool_scale, conv_w, sgu_w,
           sgu_b, ffn1_gate, ffn1_up, ffn1_down, ffn2_gate, ffn2_up, ffn2_down, norm_gains, mix_gain):
    batch, seq, _ = x_prompt.shape
    dec_batch, dec_seq, _ = x_sample.shape

    poolw_b = pool_w.astype(_BF16)
    f1 = tuple(w[0].astype(_BF16) for w in (ffn1_gate, ffn1_up, ffn1_down))
    gains = norm_gains.reshape(DEPTH * N_NORMS, 1, D_MODEL)
    mixg = mix_gain.reshape(DEPTH, 1, D_MODEL)
    pscale = pool_scale.reshape(DEPTH, 1, POOL_WIDTH)
    convw = jnp.pad(conv_w, ((0, 0), (0, CONV_HIST - conv_w.shape[1]), (0, 0)))
    sgub = jnp.repeat(jnp.swapaxes(sgu_b, 1, 2), HEAD_DIM, axis=2)
    hist_p = jnp.pad(state_pool, ((0, 0), (0, 0), (POOL_HIST - POOL_STATE, 0), (0, 0)))
    hist_z = jnp.pad(state_conv, ((0, 0), (0, 0), (CONV_HIST - CONV_STATE, 0), (0, 0)))

    yp = x_prompt.reshape(batch * seq, D_MODEL)
    ys = x_sample.reshape(dec_batch * dec_seq, D_MODEL)
    pool_p, conv_p, pool_s, conv_s, v_s = [], [], [], [], []
    for l in range(DEPTH):
        yp, win_b, wout_b, *f2 = _ffn(
            yp, gains, l, 0, *f1,
            casts=[(w_in, l), (w_out, l), (ffn2_gate, l), (ffn2_up, l), (ffn2_down, l)])
        ys = _ffn(ys, gains, l, 0, *f1)
        mix_w = (gains, mixg, win_b, wout_b, poolw_b, pscale, convw, sgu_w, sgub)
        yp, pt, zt = _mixer(yp, None, None, l, *mix_w, n_batch=batch, seq_len=seq, pos0=0)
        ys, pts, zts, vs = _mixer(ys, hist_p[l], hist_z[l], l, *mix_w,
                                  n_batch=dec_batch, seq_len=dec_seq, pos0=PAST_LEN)
        if l + 1 < DEPTH:
            yp, *f1_next = _ffn(
                yp, gains, l, 4, *f2,
                casts=[(ffn1_gate, l + 1), (ffn1_up, l + 1), (ffn1_down, l + 1)])
        else:
            yp, f1_next = _ffn(yp, gains, l, 4, *f2), None
        ys = _ffn(ys, gains, l, 4, *f2)
        f1 = f1_next
        pool_p.append(pt[:, POOL_HIST - POOL_STATE:])
        conv_p.append(zt[:, CONV_HIST - CONV_STATE:])
        pool_s.append(pts[:, POOL_HIST - POOL_STATE:])
        conv_s.append(zts[:, CONV_HIST - CONV_STATE:])
        v_s.append(vs.reshape(dec_batch, dec_seq, SGU_WIDTH))
    return (yp.reshape(batch, seq, D_MODEL), ys.reshape(dec_batch, dec_seq, D_MODEL),
            jnp.stack(pool_p), jnp.stack(conv_p), jnp.stack(pool_s), jnp.stack(conv_s), jnp.stack(v_s))
```
